```python
import jax, jax.numpy as jnp
from jax import lax
import numpy as np


D_MODEL = 1024
BATCH = 4
SEQ = 4096
DEPTH = 4

GRID_W = 64
CTX_LEN = 256
HEAD_DIM = 64
N_Q_HEADS = 16
N_KV_HEADS = 4
Q_PER_KV = N_Q_HEADS // N_KV_HEADS
D_ATTN = N_Q_HEADS * HEAD_DIM
D_KV = N_KV_HEADS * HEAD_DIM
D_LRU = D_MODEL
LRU_BLOCKS = 16
LRU_BLOCK = D_LRU // LRU_BLOCKS
CONV_W = 4
CONV_PAD_LEFT = 2
LRU_C = 8.0
ROPE_THETA = 10000.0
Q_BLOCK = 128
EPS = 1e-6
D_IN = 2 * D_ATTN + 2 * D_KV + 2 * D_LRU + 2 * D_MODEL

kernel_name = "hybrid_gqa_rglru_dit_trunk"


def rmsnorm(u, g):
    u32 = u.astype(jnp.float32)
    y = u32 * lax.rsqrt(jnp.mean(u32 * u32, axis=-1, keepdims=True) + EPS)
    return (y * g.astype(jnp.float32)).astype(u.dtype)


def modulation(cond, w_mod, b_mod):
    m = jax.nn.silu(cond) @ w_mod + b_mod
    shift, scale, gate = jnp.split(m, 3, axis=-1)
    if cond.ndim == 2:
        shift, scale, gate = shift[:, None, :], scale[:, None, :], gate[:, None, :]
    return shift, scale, gate


def split_proj(p):
    sizes = (D_ATTN, D_KV, D_KV, D_ATTN, D_LRU, D_LRU, 2 * D_MODEL)
    points = [sum(sizes[:i + 1]) for i in range(len(sizes) - 1)]
    return jnp.split(p, points, axis=-1)


def axial_rope(u, rows, cols):
    half = HEAD_DIM // 2
    quarter = half // 2
    freqs = ROPE_THETA ** (-jnp.arange(quarter, dtype=jnp.float32) / quarter)

    def rot(seg, pos):
        ang = pos.astype(jnp.float32)[:, None] * freqs
        cos = jnp.cos(ang)[None, :, None, :]
        sin = jnp.sin(ang)[None, :, None, :]
        s = seg.astype(jnp.float32)
        s1, s2 = s[..., :quarter], s[..., quarter:]
        return jnp.concatenate([s1 * cos - s2 * sin, s2 * cos + s1 * sin], axis=-1)

    out = jnp.concatenate([rot(u[..., :half], rows), rot(u[..., half:], cols)], axis=-1)
    return out.astype(u.dtype)


def gqa(q, k, v):
    s = jnp.einsum('bqkgd,bskd->bkgqs', q, k).astype(jnp.float32)
    p = jax.nn.softmax(s, axis=-1).astype(v.dtype)
    return jnp.einsum('bkgqs,bskd->bqkgd', p, v)


def latent_attention(q, k_all, v_all):
    B, n = q.shape[0], q.shape[1]
    nb = n // Q_BLOCK
    qb = q.reshape(B, nb, Q_BLOCK, N_KV_HEADS, Q_PER_KV, HEAD_DIM).swapaxes(0, 1)
    o = lax.map(lambda qblk: gqa(qblk, k_all, v_all), qb)
    return o.swapaxes(0, 1).reshape(B, n, D_ATTN)


def centred_conv(u, w, b):
    n = u.shape[1]
    up = jnp.pad(u, ((0, 0), (CONV_PAD_LEFT, CONV_W - 1 - CONV_PAD_LEFT), (0, 0)))
    out = up[:, 0:n] * w[0]
    for j in range(1, CONV_W):
        out = out + up[:, j:j + n] * w[j]
    return out + b


def block_diag(u, w):
    ub = u.reshape(u.shape[:-1] + (LRU_BLOCKS, LRU_BLOCK))
    return jnp.einsum('bnhi,hij->bnhj', ub, w).reshape(u.shape)


def lru_coeffs(u, w_gates, b_gates, lam):
    u32 = u.astype(jnp.float32)
    r = jax.nn.sigmoid(block_diag(u32, w_gates[0].astype(jnp.float32)) + b_gates[0].astype(jnp.float32))
    i = jax.nn.sigmoid(block_diag(u32, w_gates[1].astype(jnp.float32)) + b_gates[1].astype(jnp.float32))
    log_a = -LRU_C * r * jax.nn.softplus(-lam.astype(jnp.float32))
    a = jnp.exp(log_a)
    b = jnp.sqrt(-jnp.expm1(2.0 * log_a)) * (i * u32)
    return a, b


def _scan_combine(left, right):
    a1, b1 = left
    a2, b2 = right
    return a1 * a2, a2 * b1 + b2


def linear_scan(a, b, h0, reverse):
    if h0 is not None:
        idx = -1 if reverse else 0
        b = b.at[:, idx].add(a[:, idx] * h0)
    _, h = lax.associative_scan(_scan_combine, (a, b), reverse=reverse, axis=1)
    return h


def lru_branch(u_ctx, u_lat, conv_w, conv_b, w_gates, b_gates, lam, need_ctx):
    uc = centred_conv(u_ctx, conv_w, conv_b)
    ul = centred_conv(u_lat, conv_w, conv_b)
    h_ctx_dirs, h_lat_dirs = [], []
    for d, rev in enumerate((False, True)):
        a_c, b_c = lru_coeffs(uc, w_gates[d], b_gates[d], lam[d])
        h_c = linear_scan(a_c, b_c, None, rev)
        h_end = h_c[:, 0] if rev else h_c[:, -1]
        a_l, b_l = lru_coeffs(ul, w_gates[d], b_gates[d], lam[d])
        h_lat_dirs.append(linear_scan(a_l, b_l, h_end, rev))
        h_ctx_dirs.append(h_c)
    y_lat = (h_lat_dirs[0] + h_lat_dirs[1]).astype(u_lat.dtype)
    y_ctx = (h_ctx_dirs[0] + h_ctx_dirs[1]).astype(u_ctx.dtype) if need_ctx else None
    return y_ctx, y_lat


def merge_branches(o_attn, g_attn, o_lru, g_lru, g_merge, w_a_out, w_b_out, w_out):
    ya = (o_attn * jax.nn.silu(g_attn)) @ w_a_out
    yb = (o_lru * jax.nn.silu(g_lru)) @ w_b_out
    gma, gmb = jnp.split(g_merge, 2, axis=-1)
    return (jax.nn.sigmoid(gma) * ya + jax.nn.sigmoid(gmb) * yb) @ w_out


def setup_inputs(seed: int = 0) -> dict:
    key = jax.random.key(seed)
    ks = jax.random.split(key, 20)

    def nrm(k, shape, s):
        return jax.random.normal(k, shape, jnp.float32) * s

    a0 = jax.random.uniform(ks[14], (DEPTH, 2, D_LRU), jnp.float32, minval=0.9, maxval=0.999)
    return {
        "x": nrm(ks[0], (BATCH, SEQ, D_MODEL), 1.0),
        "c": nrm(ks[1], (BATCH, D_MODEL), 1.0),
        "ctx": nrm(ks[2], (BATCH, CTX_LEN, D_MODEL), 1.0),
        "c_ctx": nrm(ks[3], (D_MODEL,), 1.0),
        "norm_g": 1.0 + nrm(ks[4], (DEPTH, D_MODEL), 0.05),
        "w_mod": nrm(ks[5], (DEPTH, D_MODEL, 3 * D_MODEL), 0.5 * D_MODEL ** -0.5),
        "b_mod": nrm(ks[6], (DEPTH, 3 * D_MODEL), 0.01),
        "w_in": nrm(ks[7], (DEPTH, D_MODEL, D_IN), D_MODEL ** -0.5),
        "q_norm_g": 1.0 + nrm(ks[8], (DEPTH, HEAD_DIM), 0.05),
        "k_norm_g": 1.0 + nrm(ks[9], (DEPTH, HEAD_DIM), 0.05),
        "conv_w": nrm(ks[10], (DEPTH, CONV_W, D_LRU), CONV_W ** -0.5),
        "conv_b": nrm(ks[11], (DEPTH, D_LRU), 0.01),
        "lru_gate_w": nrm(ks[12], (DEPTH, 2, 2, LRU_BLOCKS, LRU_BLOCK, LRU_BLOCK), LRU_BLOCK ** -0.5),
        "lru_gate_b": nrm(ks[13], (DEPTH, 2, 2, D_LRU), 0.01),
        "lru_lambda": jnp.log(a0) - jnp.log1p(-a0),
        "w_a_out": nrm(ks[15], (DEPTH, D_ATTN, D_MODEL), D_ATTN ** -0.5),
        "w_b_out": nrm(ks[16], (DEPTH, D_LRU, D_MODEL), D_LRU ** -0.5),
        "w_out": nrm(ks[17], (DEPTH, D_MODEL, D_MODEL), D_MODEL ** -0.5),
    }


def reference(x, c, ctx, c_ctx, norm_g, w_mod, b_mod, w_in, q_norm_g, k_norm_g, conv_w, conv_b,
              lru_gate_w, lru_gate_b, lru_lambda, w_a_out, w_b_out, w_out):
    B, n, _ = x.shape
    L = ctx.shape[1]
    ROWS = n // GRID_W
    rows = jnp.broadcast_to(jnp.arange(ROWS)[:, None], (ROWS, GRID_W)).reshape(-1)
    cols = jnp.broadcast_to(jnp.arange(GRID_W)[None, :], (ROWS, GRID_W)).reshape(-1)
    scale = HEAD_DIM ** -0.5

    for l in range(DEPTH):
        need_ctx = l < DEPTH - 1
        sh_l, sc_l, gt_l = modulation(c, w_mod[l], b_mod[l])
        sh_c, sc_c, gt_c = modulation(c_ctx, w_mod[l], b_mod[l])
        h_lat = rmsnorm(x, norm_g[l]) * (1.0 + sc_l) + sh_l
        h_ctx = rmsnorm(ctx, norm_g[l]) * (1.0 + sc_c) + sh_c

        q_l, k_l, v_l, ga_l, u_l, gb_l, gm_l = split_proj(h_lat @ w_in[l])
        q_c, k_c, v_c, ga_c, u_c, gb_c, gm_c = split_proj(h_ctx @ w_in[l])

        q_l = axial_rope(rmsnorm(q_l.reshape(B, n, N_Q_HEADS, HEAD_DIM), q_norm_g[l]), rows, cols) * scale
        k_l = axial_rope(rmsnorm(k_l.reshape(B, n, N_KV_HEADS, HEAD_DIM), k_norm_g[l]), rows, cols)
        v_l = v_l.reshape(B, n, N_KV_HEADS, HEAD_DIM)
        k_c = rmsnorm(k_c.reshape(B, L, N_KV_HEADS, HEAD_DIM), k_norm_g[l])
        v_c = v_c.reshape(B, L, N_KV_HEADS, HEAD_DIM)
        k_all = jnp.concatenate([k_c, k_l], axis=1)
        v_all = jnp.concatenate([v_c, v_l], axis=1)
        o_lat = latent_attention(q_l, k_all, v_all)

        lru_c, lru_l = lru_branch(u_c, u_l, conv_w[l], conv_b[l], lru_gate_w[l], lru_gate_b[l],
                                  lru_lambda[l], need_ctx)

        x = x + gt_l * merge_branches(o_lat, ga_l, lru_l, gb_l, gm_l, w_a_out[l], w_b_out[l], w_out[l])

        if need_ctx:
            q_c = rmsnorm(q_c.reshape(B, L, N_Q_HEADS, HEAD_DIM), q_norm_g[l]) * scale
            o_ctx = gqa(q_c.reshape(B, L, N_KV_HEADS, Q_PER_KV, HEAD_DIM), k_c, v_c).reshape(B, L, D_ATTN)
            ctx = ctx + gt_c * merge_branches(o_ctx, ga_c, lru_c, gb_c, gm_c, w_a_out[l], w_b_out[l], w_out[l])

    return x
```

```python
import functools

import jax
import jax.numpy as jnp
from jax import lax
from jax.experimental import pallas as pl
from jax.experimental.pallas import tpu as pltpu

GRID_W = 64
HEAD_DIM = 64
N_Q_HEADS = 16
N_KV_HEADS = 4
Q_PER_KV = N_Q_HEADS // N_KV_HEADS
LRU_BLOCKS = 16
CONV_W = 4
CONV_PAD_LEFT = 2
LRU_C = 8.0
ROPE_THETA = 10000.0
EPS = 1e-6

TILE = 256
HALO = 8
LRU_GROUP = 256
VMEM_LIMIT = 56 * 1024 * 1024

F32 = jnp.float32
BF16 = jnp.bfloat16


def _params(sem, vmem=VMEM_LIMIT):
    return pltpu.CompilerParams(dimension_semantics=sem, vmem_limit_bytes=vmem)


def _const_spec(shape):
    n = len(shape)
    return pl.BlockSpec(shape, lambda *_: (0,) * n, pipeline_mode=pl.Buffered(1))


def _mod_kernel(c_ref, w_ref, b_ref, o_ref):
    s = jax.nn.silu(c_ref[...]).astype(BF16)
    o_ref[0] = jnp.dot(s, w_ref[0].astype(BF16), preferred_element_type=F32) + b_ref[0]


def _modulation(cond, w_mod, b_mod):
    depth, d, d3 = w_mod.shape
    rows = cond.shape[0]
    nchunk = d3 // d
    return pl.pallas_call(
        _mod_kernel,
        grid=(depth, nchunk),
        in_specs=[
            pl.BlockSpec((rows, d), lambda l, j: (0, 0)),
            pl.BlockSpec((1, d, d), lambda l, j: (l, 0, j)),
            pl.BlockSpec((1, 1, d), lambda l, j: (l, 0, j)),
        ],
        out_specs=pl.BlockSpec((1, rows, d), lambda l, j: (l, 0, j)),
        out_shape=jax.ShapeDtypeStruct((depth, rows, d3), F32),
        compiler_params=_params(("arbitrary", "arbitrary")),
        name="modulation",
    )(cond, w_mod, b_mod.reshape(depth, 1, d3))


def _head_norm_rope(z, g, cos, sgn):
    h, hd, tm = z.shape
    ms = jnp.mean(z * z, axis=1, keepdims=True)
    y = z * lax.rsqrt(ms + EPS) * g
    quarter = hd // 4
    y4 = y.reshape(h * 2, 2, quarter, tm)
    swapped = jnp.concatenate([y4[:, 1:2], y4[:, 0:1]], axis=1).reshape(h, hd, tm)
    return y * cos + swapped * sgn


def _in_proj_kernel(x_ref, mod_ref, ng_ref, wt_ref, wn_ref, gq_ref, gk_ref, cos_ref, sgn_ref,
                    q_ref, k_ref, v_ref, rest_ref):
    d_attn = N_Q_HEADS * HEAD_DIM
    d_kv = N_KV_HEADS * HEAD_DIM
    x = x_ref[0]
    tm = x.shape[0]
    ms = jnp.mean(x * x, axis=-1, keepdims=True)
    y = x * lax.rsqrt(ms + EPS) * ng_ref[...]
    shift = mod_ref[0, 0, 0:1, :]
    scale = mod_ref[0, 0, 1:2, :]
    h = (y * (1.0 + scale) + shift).astype(BF16)

    pt = lax.dot_general(wt_ref[...], h, (((1,), (1,)), ((), ())), preferred_element_type=F32)
    cos = cos_ref[...]
    sgn = sgn_ref[...]
    q = _head_norm_rope(pt[0:d_attn].reshape(N_Q_HEADS, HEAD_DIM, tm), gq_ref[...], cos, sgn)
    q_ref[0] = (q * (HEAD_DIM ** -0.5)).reshape(d_attn, tm).astype(BF16)
    k = _head_norm_rope(pt[d_attn:d_attn + d_kv].reshape(N_KV_HEADS, HEAD_DIM, tm), gk_ref[...], cos, sgn)
    kn = k.reshape(d_kv, tm).T
    for j in range(N_KV_HEADS):
        k_ref[0, j] = kn[:, j * HEAD_DIM:(j + 1) * HEAD_DIM].astype(BF16)
        v_ref[0, j, 0] = pt[d_attn + d_kv + j * HEAD_DIM:d_attn + d_kv + (j + 1) * HEAD_DIM].astype(BF16)

    n_rest = wn_ref.shape[1]
    step = 1024
    for c in range(0, n_rest, step):
        rest_ref[0, :, c:c + step] = jnp.dot(h, wn_ref[:, c:c + step], preferred_element_type=F32)


def _in_proj(xs, modsel, ng, wt, wn, gq, gk, cos_t, sgn_t, n_ctx_tiles):
    b, t, d = xs.shape
    nt = t // TILE
    d_attn = N_Q_HEADS * HEAD_DIM
    n_rest = wn.shape[1]
    return pl.pallas_call(
        _in_proj_kernel,
        grid=(b, nt),
        in_specs=[
            pl.BlockSpec((1, TILE, d), lambda i, j: (i, j, 0)),
            pl.BlockSpec((1, 1, 3, d), lambda i, j: (i, jnp.where(j < n_ctx_tiles, 0, 1), 0, 0)),
            _const_spec((1, d)),
            _const_spec(wt.shape),
            _const_spec(wn.shape),
            _const_spec((HEAD_DIM, 1)),
            _const_spec((HEAD_DIM, 1)),
            pl.BlockSpec((HEAD_DIM, TILE), lambda i, j: (0, j)),
            pl.BlockSpec((HEAD_DIM, TILE), lambda i, j: (0, j)),
        ],
        out_specs=[
            pl.BlockSpec((1, d_attn, TILE), lambda i, j: (i, 0, j)),
            pl.BlockSpec((1, N_KV_HEADS, TILE, HEAD_DIM), lambda i, j: (i, 0, j, 0)),
            pl.BlockSpec((1, N_KV_HEADS, 1, HEAD_DIM, TILE), lambda i, j: (i, 0, j, 0, 0)),
            pl.BlockSpec((1, TILE, n_rest), lambda i, j: (i, j, 0)),
        ],
        out_shape=[
            jax.ShapeDtypeStruct((b, d_attn, t), BF16),
            jax.ShapeDtypeStruct((b, N_KV_HEADS, t, HEAD_DIM), BF16),
            jax.ShapeDtypeStruct((b, N_KV_HEADS, nt, HEAD_DIM, TILE), BF16),
            jax.ShapeDtypeStruct((b, t, n_rest), F32),
        ],
        compiler_params=_params(("parallel", "parallel")),
        name="in_proj",
    )(xs, modsel, ng, wt, wn, gq, gk, cos_t, sgn_t)


def _attn_kernel(q_ref, k_ref, v_ref, ga_ref, o_ref, m_ref, l_ref, acc_ref, *, n_ctx_tiles, n_tiles):
    qi = pl.program_id(2)
    n_chunks = jnp.where(qi < n_ctx_tiles, n_ctx_tiles, n_tiles)
    m_ref[...] = jnp.full(m_ref.shape, -jnp.inf, F32)
    l_ref[...] = jnp.zeros(l_ref.shape, F32)
    acc_ref[...] = jnp.zeros(acc_ref.shape, F32)

    def step(c, carry):
        kc = k_ref[0, 0, pl.ds(pl.multiple_of(c * TILE, TILE), TILE), :]
        vc = v_ref[0, 0, c]
        for g in range(Q_PER_KV):
            qg = q_ref[0, g * HEAD_DIM:(g + 1) * HEAD_DIM, :]
            s = jnp.dot(kc, qg, preferred_element_type=F32)
            m_prev = m_ref[g]
            m_new = jnp.maximum(m_prev, jnp.max(s, axis=0, keepdims=True))
            alpha = jnp.exp(m_prev - m_new)
            p = jnp.exp(s - m_new)
            l_ref[g] = alpha * l_ref[g] + jnp.sum(p, axis=0, keepdims=True)
            acc_ref[g] = alpha * acc_ref[g] + jnp.dot(vc, p.astype(BF16), preferred_element_type=F32)
            m_ref[g] = m_new
        return carry

    lax.fori_loop(0, n_chunks, step, 0)
    o = jnp.concatenate([acc_ref[g] / l_ref[g] for g in range(Q_PER_KV)], axis=0)
    o_ref[0] = (o.T * jax.nn.silu(ga_ref[0])).astype(BF16)


def _attention(qt, k4, vt, rest, ga_col, n_ctx_tiles):
    b, d_attn, t = qt.shape
    nt = t // TILE
    width = Q_PER_KV * HEAD_DIM
    kern = functools.partial(_attn_kernel, n_ctx_tiles=n_ctx_tiles, n_tiles=nt)
    return pl.pallas_call(
        kern,
        grid=(b, N_KV_HEADS, nt),
        in_specs=[
            pl.BlockSpec((1, width, TILE), lambda i, j, q: (i, j, q)),
            pl.BlockSpec((1, 1, t, HEAD_DIM), lambda i, j, q: (i, j, 0, 0)),
            pl.BlockSpec((1, 1, nt, HEAD_DIM, TILE), lambda i, j, q: (i, j, 0, 0, 0)),
            pl.BlockSpec((1, TILE, width), lambda i, j, q: (i, q, ga_col // width + j)),
        ],
        out_specs=pl.BlockSpec((1, TILE, width), lambda i, j, q: (i, q, j)),
        out_shape=jax.ShapeDtypeStruct((b, t, d_attn), BF16),
        scratch_shapes=[
            pltpu.VMEM((Q_PER_KV, 1, TILE), F32),
            pltpu.VMEM((Q_PER_KV, 1, TILE), F32),
            pltpu.VMEM((Q_PER_KV, HEAD_DIM, TILE), F32),
        ],
        compiler_params=_params(("parallel", "parallel", "arbitrary")),
        name="attention",
    )(qt, k4, vt, rest)


def _softplus(x):
    return jnp.maximum(x, 0.0) + jnp.log1p(jnp.exp(-jnp.abs(x)))


def _block_scan(a, b, reverse):
    n = a.shape[0]
    row = lax.broadcasted_iota(jnp.int32, a.shape, 0)
    s = 1
    while s < n:
        if reverse:
            keep = row < n - s
            shift = n - s
        else:
            keep = row >= s
            shift = s
        a_sh = jnp.where(keep, pltpu.roll(a, shift, 0), 1.0)
        b_sh = jnp.where(keep, pltpu.roll(b, shift, 0), 0.0)
        b = a * b_sh + b
        a = a * a_sh
        s *= 2
    return a, b


def _lru_tile_index(i, n_ctx_tiles, n_tiles, reverse):
    if not reverse:
        return i
    return jnp.where(i < n_ctx_tiles, n_ctx_tiles - 1 - i, n_tiles - 1 - (i - n_ctx_tiles))


def _lru_kernel(*refs, n_ctx_tiles, n_tiles, reverse):
    if reverse:
        (u_ref, up_ref, un_ref, cw_ref, cb_ref, wg_ref, bg_ref, lam_ref, hf_ref, gb_ref,
         o_ref, carry_ref) = refs
    else:
        (u_ref, up_ref, un_ref, cw_ref, cb_ref, wg_ref, bg_ref, lam_ref,
         o_ref, carry_ref) = refs
    i = pl.program_id(1)
    tb = _lru_tile_index(i, n_ctx_tiles, n_tiles, reverse)

    @pl.when(i == 0)
    def _():
        carry_ref[...] = jnp.zeros(carry_ref.shape, F32)

    has_prev = jnp.logical_and(tb != 0, tb != n_ctx_tiles)
    has_next = jnp.logical_and(tb != n_ctx_tiles - 1, tb != n_tiles - 1)
    u = u_ref[0]
    prev = jnp.where(has_prev, up_ref[0], 0.0)
    nxt = jnp.where(has_next, un_ref[0], 0.0)
    ext = jnp.concatenate([prev, u, nxt], axis=0)
    tm = u.shape[0]
    base = HALO - CONV_PAD_LEFT
    uc = ext[base:base + tm] * cw_ref[0:1, :]
    for j in range(1, CONV_W):
        uc = uc + ext[base + j:base + j + tm] * cw_ref[j:j + 1, :]
    uc = uc + cb_ref[...]
    ucb = uc.astype(BF16)

    sp = _softplus(-lam_ref[...])
    n_groups = uc.shape[1] // LRU_GROUP
    for cg in range(n_groups):
        lo, hi = cg * LRU_GROUP, (cg + 1) * LRU_GROUP
        pre = jnp.dot(ucb[:, lo:hi], wg_ref[cg], preferred_element_type=F32)
        r = jax.nn.sigmoid(pre[:, :LRU_GROUP] + bg_ref[0:1, lo:hi])
        gi = jax.nn.sigmoid(pre[:, LRU_GROUP:] + bg_ref[1:2, lo:hi])
        log_a = (-LRU_C) * r * sp[:, lo:hi]
        a = jnp.exp(log_a)
        th = jnp.tanh(log_a)
        bb = jnp.sqrt(-2.0 * th / (1.0 - th)) * (gi * uc[:, lo:hi])
        a_cum, h0 = _block_scan(a, bb, reverse)
        h = h0 + a_cum * carry_ref[:, lo:hi]
        edge = 0 if reverse else tm - 1
        carry_ref[:, lo:hi] = h[edge:edge + 1, :]
        if reverse:
            y = hf_ref[0, :, lo:hi] + h
            o_ref[0, :, lo:hi] = (y * jax.nn.silu(gb_ref[0, :, lo:hi])).astype(BF16)
        else:
            o_ref[0, :, lo:hi] = h


def _lru(rest, u_col, gb_col, conv_w, conv_b, wg, bg, lam, hf, n_ctx_tiles, reverse):
    b, t, _ = rest.shape
    d = conv_w.shape[1]
    nt = t // TILE
    per_tile = TILE // HALO
    n_halo_blocks = t // HALO
    tile = functools.partial(_lru_tile_index, n_ctx_tiles=n_ctx_tiles, n_tiles=nt, reverse=reverse)
    ucol = u_col // d
    in_specs = [
        pl.BlockSpec((1, TILE, d), lambda i, j: (i, tile(j), ucol)),
        pl.BlockSpec((1, HALO, d), lambda i, j: (i, jnp.maximum(tile(j) * per_tile - 1, 0), ucol)),
        pl.BlockSpec((1, HALO, d),
                     lambda i, j: (i, jnp.minimum((tile(j) + 1) * per_tile, n_halo_blocks - 1), ucol)),
        _const_spec(conv_w.shape),
        _const_spec((1, d)),
        _const_spec(wg.shape),
        _const_spec(bg.shape),
        _const_spec((1, d)),
    ]
    args = [rest, rest, rest, conv_w, conv_b, wg, bg, lam]
    if reverse:
        in_specs += [
            pl.BlockSpec((1, TILE, d), lambda i, j: (i, tile(j), 0)),
            pl.BlockSpec((1, TILE, d), lambda i, j: (i, tile(j), gb_col // d)),
        ]
        args += [hf, rest]
    kern = functools.partial(_lru_kernel, n_ctx_tiles=n_ctx_tiles, n_tiles=nt, reverse=reverse)
    return pl.pallas_call(
        kern,
        grid=(b, nt),
        in_specs=in_specs,
        out_specs=pl.BlockSpec((1, TILE, d), lambda i, j: (i, tile(j), 0)),
        out_shape=jax.ShapeDtypeStruct((b, t, d), BF16 if reverse else F32),
        scratch_shapes=[pltpu.VMEM((1, d), F32)],
        compiler_params=_params(("parallel", "arbitrary")),
        name="lru_rev" if reverse else "lru_fwd",
    )(*args)


def _merge_kernel(oa_ref, ob_ref, gm_ref, x_ref, mod_ref, wa_ref, wb_ref, wo_ref, o_ref):
    d = x_ref.shape[2]
    ya = jnp.dot(oa_ref[0], wa_ref[...], preferred_element_type=F32)
    yb = jnp.dot(ob_ref[0], wb_ref[...], preferred_element_type=F32)
    mix = jax.nn.sigmoid(gm_ref[0, :, 0:d]) * ya + jax.nn.sigmoid(gm_ref[0, :, d:2 * d]) * yb
    out = jnp.dot(mix.astype(BF16), wo_ref[...], preferred_element_type=F32)
    o_ref[0] = x_ref[0] + mod_ref[0, 0, 2:3, :] * out


def _merge(oa, ob, rest, xs, modsel, wa, wb, wo, n_ctx_tiles):
    b, t, d = xs.shape
    nt = t // TILE
    return pl.pallas_call(
        _merge_kernel,
        grid=(b, nt),
        in_specs=[
            pl.BlockSpec((1, TILE, d), lambda i, j: (i, j, 0)),
            pl.BlockSpec((1, TILE, d), lambda i, j: (i, j, 0)),
            pl.BlockSpec((1, TILE, 2 * d), lambda i, j: (i, j, 0)),
            pl.BlockSpec((1, TILE, d), lambda i, j: (i, j, 0)),
            pl.BlockSpec((1, 1, 3, d), lambda i, j: (i, jnp.where(j < n_ctx_tiles, 0, 1), 0, 0)),
            _const_spec(wa.shape),
            _const_spec(wb.shape),
            _const_spec(wo.shape),
        ],
        out_specs=pl.BlockSpec((1, TILE, d), lambda i, j: (i, j, 0)),
        out_shape=jax.ShapeDtypeStruct((b, t, d), F32),
        compiler_params=_params(("parallel", "parallel")),
        name="merge",
    )(oa, ob, rest, xs, modsel, wa, wb, wo)


def _rope_tables(n, n_ctx):
    quarter = HEAD_DIM // 4
    freqs = ROPE_THETA ** (-jnp.arange(quarter, dtype=F32) / quarter)
    pos = jnp.arange(n)
    ang_r = (pos // GRID_W).astype(F32)[None, :] * freqs[:, None]
    ang_c = (pos % GRID_W).astype(F32)[None, :] * freqs[:, None]
    cos = jnp.concatenate([jnp.cos(ang_r)] * 2 + [jnp.cos(ang_c)] * 2, axis=0)
    sgn = jnp.concatenate([-jnp.sin(ang_r), jnp.sin(ang_r), -jnp.sin(ang_c), jnp.sin(ang_c)], axis=0)
    cos = jnp.concatenate([jnp.ones((HEAD_DIM, n_ctx), F32), cos], axis=1)
    sgn = jnp.concatenate([jnp.zeros((HEAD_DIM, n_ctx), F32), sgn], axis=1)
    return cos, sgn


def _gate_weights(w):
    per = LRU_GROUP // (w.shape[-1])
    n_groups = LRU_BLOCKS // per
    blk = w.shape[-1]
    wg = w.reshape(2, n_groups, per, blk, blk)
    eye = jnp.eye(per, dtype=w.dtype)
    dense = jnp.einsum('gnpij,pq->gnpiqj', wg, eye).reshape(2, n_groups, per * blk, per * blk)
    return jnp.concatenate([dense[0], dense[1]], axis=-1).astype(BF16)


def kernel(x, c, ctx, c_ctx, norm_g, w_mod, b_mod, w_in, q_norm_g, k_norm_g, conv_w, conv_b,
           lru_gate_w, lru_gate_b, lru_lambda, w_a_out, w_b_out, w_out):
    bsz, n, d = x.shape
    n_ctx = ctx.shape[1]
    depth = w_in.shape[0]
    assert n % TILE == 0 and n_ctx % TILE == 0 and n % GRID_W == 0
    d_attn = N_Q_HEADS * HEAD_DIM
    d_kv = N_KV_HEADS * HEAD_DIM
    d_lru = conv_w.shape[-1]
    assert d_lru == d and d_attn == d and w_in.shape[2] == 2 * d_attn + 2 * d_kv + 2 * d_lru + 2 * d
    n_ctx_tiles = n_ctx // TILE

    xs = jnp.concatenate([ctx, x], axis=1)
    rows = -(-(bsz + 1) // 8) * 8
    cond = jnp.zeros((rows, d), F32).at[:bsz].set(c).at[bsz].set(c_ctx)
    mod = _modulation(cond, w_mod, b_mod)
    cos_t, sgn_t = _rope_tables(n, n_ctx)

    o_q, o_k, o_v = 0, d_attn, d_attn + d_kv
    o_ga = d_attn + 2 * d_kv
    o_u = o_ga + d_attn
    o_gb = o_u + d_lru
    o_gm = o_gb + d_lru
    ga_col, u_col, gb_col = 2 * d, 2 * d + d_attn, 2 * d + d_attn + d_lru

    for l in range(depth):
        m3 = mod[l].reshape(rows, 3, d)
        modsel = jnp.stack([jnp.broadcast_to(m3[bsz], (bsz, 3, d)), m3[:bsz]], axis=1)
        w = w_in[l]
        wt = w[:, o_q:o_ga].T.astype(BF16)
        wn = jnp.concatenate([w[:, o_gm:], w[:, o_ga:o_u], w[:, o_u:o_gb], w[:, o_gb:o_gm]],
                             axis=1).astype(BF16)
        qt, k4, vt, rest = _in_proj(xs, modsel, norm_g[l][None, :], wt, wn,
                                    q_norm_g[l][:, None], k_norm_g[l][:, None], cos_t, sgn_t, n_ctx_tiles)
        oa = _attention(qt, k4, vt, rest, ga_col, n_ctx_tiles)
        hf = None
        for dr, rev in enumerate((False, True)):
            hf = _lru(rest, u_col, gb_col, conv_w[l], conv_b[l][None, :], _gate_weights(lru_gate_w[l, dr]),
                      lru_gate_b[l, dr], lru_lambda[l, dr][None, :], hf, n_ctx_tiles, rev)
        xs = _merge(oa, hf, rest, xs, modsel, w_a_out[l].astype(BF16), w_b_out[l].astype(BF16),
                    w_out[l].astype(BF16), n_ctx_tiles)
    return xs[:, n_ctx:]
```

```python
import functools

import jax
import jax.numpy as jnp
from jax import lax
from jax.experimental import pallas as pl
from jax.experimental.pallas import tpu as pltpu

GRID_W = 64
HEAD_DIM = 64
N_Q_HEADS = 16
N_KV_HEADS = 4
Q_PER_KV = N_Q_HEADS // N_KV_HEADS
LRU_BLOCKS = 16
CONV_W = 4
CONV_PAD_LEFT = 2
LRU_C = 8.0
ROPE_THETA = 10000.0
EPS = 1e-6

TILE = 256
HALO = 8
LRU_GROUP = 256
SUM_ROWS = 16
LOG2_E = 1.4426950408889634
VMEM_LIMIT = 56 * 1024 * 1024

F32 = jnp.float32
BF16 = jnp.bfloat16


def _params(sem, vmem=VMEM_LIMIT):
    return pltpu.CompilerParams(dimension_semantics=sem, vmem_limit_bytes=vmem)


def _const_spec(shape):
    n = len(shape)
    return pl.BlockSpec(shape, lambda *_: (0,) * n, pipeline_mode=pl.Buffered(1))


def _mod_kernel(c_ref, w_ref, b_ref, o_ref):
    s = jax.nn.silu(c_ref[...]).astype(BF16)
    o_ref[0] = jnp.dot(s, w_ref[0].astype(BF16), preferred_element_type=F32) + b_ref[0]


def _modulation(cond, w_mod, b_mod):
    depth, d, d3 = w_mod.shape
    rows = cond.shape[0]
    nchunk = d3 // d
    return pl.pallas_call(
        _mod_kernel,
        grid=(depth, nchunk),
        in_specs=[
            pl.BlockSpec((rows, d), lambda l, j: (0, 0)),
            pl.BlockSpec((1, d, d), lambda l, j: (l, 0, j)),
            pl.BlockSpec((1, 1, d), lambda l, j: (l, 0, j)),
        ],
        out_specs=pl.BlockSpec((1, rows, d), lambda l, j: (l, 0, j)),
        out_shape=jax.ShapeDtypeStruct((depth, rows, d3), F32),
        compiler_params=_params(("arbitrary", "arbitrary")),
        name="modulation",
    )(cond, w_mod, b_mod.reshape(depth, 1, d3))


def _head_norm_rope(z, g, cos, sgn):
    h, hd, tm = z.shape
    ms = jnp.mean(z * z, axis=1, keepdims=True)
    y = z * lax.rsqrt(ms + EPS) * g
    quarter = hd // 4
    y4 = y.reshape(h * 2, 2, quarter, tm)
    swapped = jnp.concatenate([y4[:, 1:2], y4[:, 0:1]], axis=1).reshape(h, hd, tm)
    return y * cos + swapped * sgn


def _in_proj_kernel(x_ref, mod_ref, ng_ref, wt_ref, wn_ref, gq_ref, gk_ref, cos_ref, sgn_ref,
                    q_ref, k_ref, v_ref, rest_ref):
    d_attn = N_Q_HEADS * HEAD_DIM
    d_kv = N_KV_HEADS * HEAD_DIM
    x = x_ref[0]
    tm = x.shape[0]
    ms = jnp.mean(x * x, axis=-1, keepdims=True)
    y = x * lax.rsqrt(ms + EPS) * ng_ref[...]
    shift = mod_ref[0, 0, 0:1, :]
    scale = mod_ref[0, 0, 1:2, :]
    h = (y * (1.0 + scale) + shift).astype(BF16)

    pt = lax.dot_general(wt_ref[...], h, (((1,), (1,)), ((), ())), preferred_element_type=F32)
    cos = cos_ref[...]
    sgn = sgn_ref[...]
    q = _head_norm_rope(pt[0:d_attn].reshape(N_Q_HEADS, HEAD_DIM, tm), gq_ref[...], cos, sgn)
    q_ref[0] = (q * (HEAD_DIM ** -0.5 * LOG2_E)).reshape(d_attn, tm).astype(BF16)
    k = _head_norm_rope(pt[d_attn:d_attn + d_kv].reshape(N_KV_HEADS, HEAD_DIM, tm), gk_ref[...], cos, sgn)
    kn = k.reshape(d_kv, tm).T
    for j in range(N_KV_HEADS):
        k_ref[0, j] = kn[:, j * HEAD_DIM:(j + 1) * HEAD_DIM].astype(BF16)
        v_ref[0, j, 0] = pt[d_attn + d_kv + j * HEAD_DIM:d_attn + d_kv + (j + 1) * HEAD_DIM].astype(BF16)

    n_rest = wn_ref.shape[1]
    step = 1024
    for c in range(0, n_rest, step):
        rest_ref[0, :, c:c + step] = jnp.dot(h, wn_ref[:, c:c + step], preferred_element_type=F32)


def _in_proj(xs, modsel, ng, wt, wn, gq, gk, cos_t, sgn_t, n_ctx_tiles):
    b, t, d = xs.shape
    nt = t // TILE
    d_attn = N_Q_HEADS * HEAD_DIM
    n_rest = wn.shape[1]
    return pl.pallas_call(
        _in_proj_kernel,
        grid=(b, nt),
        in_specs=[
            pl.BlockSpec((1, TILE, d), lambda i, j: (i, j, 0)),
            pl.BlockSpec((1, 1, 3, d), lambda i, j: (i, jnp.where(j < n_ctx_tiles, 0, 1), 0, 0)),
            _const_spec((1, d)),
            _const_spec(wt.shape),
            _const_spec(wn.shape),
            _const_spec((HEAD_DIM, 1)),
            _const_spec((HEAD_DIM, 1)),
            pl.BlockSpec((HEAD_DIM, TILE), lambda i, j: (0, j)),
            pl.BlockSpec((HEAD_DIM, TILE), lambda i, j: (0, j)),
        ],
        out_specs=[
            pl.BlockSpec((1, d_attn, TILE), lambda i, j: (i, 0, j)),
            pl.BlockSpec((1, N_KV_HEADS, TILE, HEAD_DIM), lambda i, j: (i, 0, j, 0)),
            pl.BlockSpec((1, N_KV_HEADS, 1, HEAD_DIM, TILE), lambda i, j: (i, 0, j, 0, 0)),
            pl.BlockSpec((1, TILE, n_rest), lambda i, j: (i, j, 0)),
        ],
        out_shape=[
            jax.ShapeDtypeStruct((b, d_attn, t), BF16),
            jax.ShapeDtypeStruct((b, N_KV_HEADS, t, HEAD_DIM), BF16),
            jax.ShapeDtypeStruct((b, N_KV_HEADS, nt, HEAD_DIM, TILE), BF16),
            jax.ShapeDtypeStruct((b, t, n_rest), F32),
        ],
        compiler_params=_params(("parallel", "parallel")),
        name="in_proj",
    )(xs, modsel, ng, wt, wn, gq, gk, cos_t, sgn_t)


def _attn_kernel(q_ref, k_ref, v_ref, ga_ref, o_ref, s_ref, mx_ref, m_ref, acc_ref,
                 *, n_ctx_tiles, n_tiles):
    qi = pl.program_id(2)
    m_ref[...] = jnp.full(m_ref.shape, -jnp.inf, F32)
    acc_ref[...] = jnp.zeros(acc_ref.shape, F32)
    ones_rows = jnp.ones((SUM_ROWS, TILE), BF16)

    def scores(c, slot):
        kc = k_ref[0, 0, c * TILE:(c + 1) * TILE, :]
        for g in range(Q_PER_KV):
            qg = q_ref[0, g * HEAD_DIM:(g + 1) * HEAD_DIM, :]
            s = jnp.dot(kc, qg, preferred_element_type=F32)
            s_ref[slot, g] = s
            mx_ref[slot, g] = jnp.max(s, axis=0, keepdims=True)

    def update(c, slot):
        vc = jnp.concatenate([v_ref[0, 0, c], ones_rows], axis=0)
        for g in range(Q_PER_KV):
            m_prev = m_ref[g]
            m_new = jnp.maximum(m_prev, mx_ref[slot, g])
            alpha = jnp.exp2(m_prev - m_new)
            p = jnp.exp2(s_ref[slot, g] - m_new)
            acc_ref[g] = alpha * acc_ref[g] + jnp.dot(vc, p.astype(BF16), preferred_element_type=F32)
            m_ref[g] = m_new

    def run(n_chunks):
        scores(0, 0)
        for c in range(n_chunks):
            if c + 1 < n_chunks:
                scores(c + 1, (c + 1) % 2)
            update(c, c % 2)
        o = jnp.concatenate([acc_ref[g, 0:HEAD_DIM] / acc_ref[g, HEAD_DIM:HEAD_DIM + 1]
                             for g in range(Q_PER_KV)], axis=0)
        o_ref[0] = (o.T * jax.nn.silu(ga_ref[0])).astype(BF16)

    @pl.when(qi < n_ctx_tiles)
    def _():
        run(n_ctx_tiles)

    @pl.when(qi >= n_ctx_tiles)
    def _():
        run(n_tiles)


def _attention(qt, k4, vt, rest, ga_col, n_ctx_tiles):
    b, d_attn, t = qt.shape
    nt = t // TILE
    width = Q_PER_KV * HEAD_DIM
    kern = functools.partial(_attn_kernel, n_ctx_tiles=n_ctx_tiles, n_tiles=nt)
    return pl.pallas_call(
        kern,
        grid=(b, N_KV_HEADS, nt),
        in_specs=[
            pl.BlockSpec((1, width, TILE), lambda i, j, q: (i, j, q)),
            pl.BlockSpec((1, 1, t, HEAD_DIM), lambda i, j, q: (i, j, 0, 0)),
            pl.BlockSpec((1, 1, nt, HEAD_DIM, TILE), lambda i, j, q: (i, j, 0, 0, 0)),
            pl.BlockSpec((1, TILE, width), lambda i, j, q: (i, q, ga_col // width + j)),
        ],
        out_specs=pl.BlockSpec((1, TILE, width), lambda i, j, q: (i, q, j)),
        out_shape=jax.ShapeDtypeStruct((b, t, d_attn), BF16),
        scratch_shapes=[
            pltpu.VMEM((2, Q_PER_KV, TILE, TILE), F32),
            pltpu.VMEM((2, Q_PER_KV, 1, TILE), F32),
            pltpu.VMEM((Q_PER_KV, 1, TILE), F32),
            pltpu.VMEM((Q_PER_KV, HEAD_DIM + SUM_ROWS, TILE), F32),
        ],
        compiler_params=_params(("parallel", "parallel", "arbitrary")),
        name="attention",
    )(qt, k4, vt, rest)


def _softplus(x):
    return jnp.maximum(x, 0.0) + jnp.log1p(jnp.exp(-jnp.abs(x)))


def _block_scan(a, b, reverse):
    n = a.shape[0]
    row = lax.broadcasted_iota(jnp.int32, a.shape, 0)
    s = 1
    while s < n:
        if reverse:
            keep = row < n - s
            shift = n - s
        else:
            keep = row >= s
            shift = s
        a_sh = jnp.where(keep, pltpu.roll(a, shift, 0), 1.0)
        b_sh = jnp.where(keep, pltpu.roll(b, shift, 0), 0.0)
        b = a * b_sh + b
        a = a * a_sh
        s *= 2
    return a, b


def _lru_tile_index(i, n_ctx_tiles, n_tiles, reverse):
    if not reverse:
        return i
    return jnp.where(i < n_ctx_tiles, n_ctx_tiles - 1 - i, n_tiles - 1 - (i - n_ctx_tiles))


def _lru_kernel(*refs, n_ctx_tiles, n_tiles, reverse):
    if reverse:
        (u_ref, up_ref, un_ref, cw_ref, cb_ref, wg_ref, bg_ref, lam_ref, hf_ref, gb_ref,
         o_ref, carry_ref) = refs
    else:
        (u_ref, up_ref, un_ref, cw_ref, cb_ref, wg_ref, bg_ref, lam_ref,
         o_ref, carry_ref) = refs
    i = pl.program_id(1)
    tb = _lru_tile_index(i, n_ctx_tiles, n_tiles, reverse)

    @pl.when(i == 0)
    def _():
        carry_ref[...] = jnp.zeros(carry_ref.shape, F32)

    has_prev = jnp.logical_and(tb != 0, tb != n_ctx_tiles)
    has_next = jnp.logical_and(tb != n_ctx_tiles - 1, tb != n_tiles - 1)
    u = u_ref[0]
    prev = jnp.where(has_prev, up_ref[0], 0.0)
    nxt = jnp.where(has_next, un_ref[0], 0.0)
    ext = jnp.concatenate([prev, u, nxt], axis=0)
    tm = u.shape[0]
    base = HALO - CONV_PAD_LEFT
    uc = ext[base:base + tm] * cw_ref[0:1, :]
    for j in range(1, CONV_W):
        uc = uc + ext[base + j:base + j + tm] * cw_ref[j:j + 1, :]
    uc = uc + cb_ref[...]
    ucb = uc.astype(BF16)

    sp = _softplus(-lam_ref[...])
    n_groups = uc.shape[1] // LRU_GROUP
    for cg in range(n_groups):
        lo, hi = cg * LRU_GROUP, (cg + 1) * LRU_GROUP
        pre = jnp.dot(ucb[:, lo:hi], wg_ref[cg], preferred_element_type=F32)
        r = jax.nn.sigmoid(pre[:, :LRU_GROUP] + bg_ref[0:1, lo:hi])
        gi = jax.nn.sigmoid(pre[:, LRU_GROUP:] + bg_ref[1:2, lo:hi])
        log_a = (-LRU_C) * r * sp[:, lo:hi]
        a = jnp.exp(log_a)
        th = jnp.tanh(log_a)
        bb = jnp.sqrt(-2.0 * th / (1.0 - th)) * (gi * uc[:, lo:hi])
        a_cum, h0 = _block_scan(a, bb, reverse)
        h = h0 + a_cum * carry_ref[:, lo:hi]
        edge = 0 if reverse else tm - 1
        carry_ref[:, lo:hi] = h[edge:edge + 1, :]
        if reverse:
            y = hf_ref[0, :, lo:hi] + h
            o_ref[0, :, lo:hi] = (y * jax.nn.silu(gb_ref[0, :, lo:hi])).astype(BF16)
        else:
            o_ref[0, :, lo:hi] = h


def _lru(rest, u_col, gb_col, conv_w, conv_b, wg, bg, lam, hf, n_ctx_tiles, reverse):
    b, t, _ = rest.shape
    d = conv_w.shape[1]
    nt = t // TILE
    per_tile = TILE // HALO
    n_halo_blocks = t // HALO
    tile = functools.partial(_lru_tile_index, n_ctx_tiles=n_ctx_tiles, n_tiles=nt, reverse=reverse)
    ucol = u_col // d
    in_specs = [
        pl.BlockSpec((1, TILE, d), lambda i, j: (i, tile(j), ucol)),
        pl.BlockSpec((1, HALO, d), lambda i, j: (i, jnp.maximum(tile(j) * per_tile - 1, 0), ucol)),
        pl.BlockSpec((1, HALO, d),
                     lambda i, j: (i, jnp.minimum((tile(j) + 1) * per_tile, n_halo_blocks - 1), ucol)),
        _const_spec(conv_w.shape),
        _const_spec((1, d)),
        _const_spec(wg.shape),
        _const_spec(bg.shape),
        _const_spec((1, d)),
    ]
    args = [rest, rest, rest, conv_w, conv_b, wg, bg, lam]
    if reverse:
        in_specs += [
            pl.BlockSpec((1, TILE, d), lambda i, j: (i, tile(j), 0)),
            pl.BlockSpec((1, TILE, d), lambda i, j: (i, tile(j), gb_col // d)),
        ]
        args += [hf, rest]
    kern = functools.partial(_lru_kernel, n_ctx_tiles=n_ctx_tiles, n_tiles=nt, reverse=reverse)
    return pl.pallas_call(
        kern,
        grid=(b, nt),
        in_specs=in_specs,
        out_specs=pl.BlockSpec((1, TILE, d), lambda i, j: (i, tile(j), 0)),
        out_shape=jax.ShapeDtypeStruct((b, t, d), BF16 if reverse else F32),
        scratch_shapes=[pltpu.VMEM((1, d), F32)],
        compiler_params=_params(("parallel", "arbitrary")),
        name="lru_rev" if reverse else "lru_fwd",
    )(*args)


def _merge_kernel(oa_ref, ob_ref, gm_ref, x_ref, mod_ref, wa_ref, wb_ref, wo_ref, o_ref):
    d = x_ref.shape[2]
    ya = jnp.dot(oa_ref[0], wa_ref[...], preferred_element_type=F32)
    yb = jnp.dot(ob_ref[0], wb_ref[...], preferred_element_type=F32)
    mix = jax.nn.sigmoid(gm_ref[0, :, 0:d]) * ya + jax.nn.sigmoid(gm_ref[0, :, d:2 * d]) * yb
    out = jnp.dot(mix.astype(BF16), wo_ref[...], preferred_element_type=F32)
    o_ref[0] = x_ref[0] + mod_ref[0, 0, 2:3, :] * out


def _merge(oa, ob, rest, xs, modsel, wa, wb, wo, n_ctx_tiles):
    b, t, d = xs.shape
    nt = t // TILE
    return pl.pallas_call(
        _merge_kernel,
        grid=(b, nt),
        in_specs=[
            pl.BlockSpec((1, TILE, d), lambda i, j: (i, j, 0)),
            pl.BlockSpec((1, TILE, d), lambda i, j: (i, j, 0)),
            pl.BlockSpec((1, TILE, 2 * d), lambda i, j: (i, j, 0)),
            pl.BlockSpec((1, TILE, d), lambda i, j: (i, j, 0)),
            pl.BlockSpec((1, 1, 3, d), lambda i, j: (i, jnp.where(j < n_ctx_tiles, 0, 1), 0, 0)),
            _const_spec(wa.shape),
            _const_spec(wb.shape),
            _const_spec(wo.shape),
        ],
        out_specs=pl.BlockSpec((1, TILE, d), lambda i, j: (i, j, 0)),
        out_shape=jax.ShapeDtypeStruct((b, t, d), F32),
        compiler_params=_params(("parallel", "parallel")),
        name="merge",
    )(oa, ob, rest, xs, modsel, wa, wb, wo)


def _rope_tables(n, n_ctx):
    quarter = HEAD_DIM // 4
    freqs = ROPE_THETA ** (-jnp.arange(quarter, dtype=F32) / quarter)
    pos = jnp.arange(n)
    ang_r = (pos // GRID_W).astype(F32)[None, :] * freqs[:, None]
    ang_c = (pos % GRID_W).astype(F32)[None, :] * freqs[:, None]
    cos = jnp.concatenate([jnp.cos(ang_r)] * 2 + [jnp.cos(ang_c)] * 2, axis=0)
    sgn = jnp.concatenate([-jnp.sin(ang_r), jnp.sin(ang_r), -jnp.sin(ang_c), jnp.sin(ang_c)], axis=0)
    cos = jnp.concatenate([jnp.ones((HEAD_DIM, n_ctx), F32), cos], axis=1)
    sgn = jnp.concatenate([jnp.zeros((HEAD_DIM, n_ctx), F32), sgn], axis=1)
    return cos, sgn


def _gate_weights(w):
    per = LRU_GROUP // (w.shape[-1])
    n_groups = LRU_BLOCKS // per
    blk = w.shape[-1]
    wg = w.reshape(2, n_groups, per, blk, blk)
    eye = jnp.eye(per, dtype=w.dtype)
    dense = jnp.einsum('gnpij,pq->gnpiqj', wg, eye).reshape(2, n_groups, per * blk, per * blk)
    return jnp.concatenate([dense[0], dense[1]], axis=-1).astype(BF16)


def kernel(x, c, ctx, c_ctx, norm_g, w_mod, b_mod, w_in, q_norm_g, k_norm_g, conv_w, conv_b,
           lru_gate_w, lru_gate_b, lru_lambda, w_a_out, w_b_out, w_out):
    bsz, n, d = x.shape
    n_ctx = ctx.shape[1]
    depth = w_in.shape[0]
    assert n % TILE == 0 and n_ctx % TILE == 0 and n % GRID_W == 0
    d_attn = N_Q_HEADS * HEAD_DIM
    d_kv = N_KV_HEADS * HEAD_DIM
    d_lru = conv_w.shape[-1]
    assert d_lru == d and d_attn == d and w_in.shape[2] == 2 * d_attn + 2 * d_kv + 2 * d_lru + 2 * d
    n_ctx_tiles = n_ctx // TILE

    xs = jnp.concatenate([ctx, x], axis=1)
    rows = -(-(bsz + 1) // 8) * 8
    cond = jnp.zeros((rows, d), F32).at[:bsz].set(c).at[bsz].set(c_ctx)
    mod = _modulation(cond, w_mod, b_mod)
    cos_t, sgn_t = _rope_tables(n, n_ctx)

    o_q, o_k, o_v = 0, d_attn, d_attn + d_kv
    o_ga = d_attn + 2 * d_kv
    o_u = o_ga + d_attn
    o_gb = o_u + d_lru
    o_gm = o_gb + d_lru
    ga_col, u_col, gb_col = 2 * d, 2 * d + d_attn, 2 * d + d_attn + d_lru

    for l in range(depth):
        m3 = mod[l].reshape(rows, 3, d)
        modsel = jnp.stack([jnp.broadcast_to(m3[bsz], (bsz, 3, d)), m3[:bsz]], axis=1)
        w = w_in[l]
        wt = w[:, o_q:o_ga].T.astype(BF16)
        wn = jnp.concatenate([w[:, o_gm:], w[:, o_ga:o_u], w[:, o_u:o_gb], w[:, o_gb:o_gm]],
                             axis=1).astype(BF16)
        qt, k4, vt, rest = _in_proj(xs, modsel, norm_g[l][None, :], wt, wn,
                                    q_norm_g[l][:, None], k_norm_g[l][:, None], cos_t, sgn_t, n_ctx_tiles)
        oa = _attention(qt, k4, vt, rest, ga_col, n_ctx_tiles)
        hf = None
        for dr, rev in enumerate((False, True)):
            hf = _lru(rest, u_col, gb_col, conv_w[l], conv_b[l][None, :], _gate_weights(lru_gate_w[l, dr]),
                      lru_gate_b[l, dr], lru_lambda[l, dr][None, :], hf, n_ctx_tiles, rev)
        xs = _merge(oa, hf, rest, xs, modsel, w_a_out[l].astype(BF16), w_b_out[l].astype(BF16),
                    w_out[l].astype(BF16), n_ctx_tiles)
    return xs[:, n_ctx:]
```

```python
import functools

import jax
import jax.numpy as jnp
from jax import lax
from jax.experimental import pallas as pl
from jax.experimental.pallas import tpu as pltpu

GRID_W = 64
HEAD_DIM = 64
N_Q_HEADS = 16
N_KV_HEADS = 4
Q_PER_KV = N_Q_HEADS // N_KV_HEADS
LRU_BLOCKS = 16
CONV_W = 4
CONV_PAD_LEFT = 2
LRU_C = 8.0
ROPE_THETA = 10000.0
EPS = 1e-6

TILE = 256
HALO = 8
LANES = 128
LRU_GROUP = 256
EXT_ROWS = TILE + 2 * HALO
LRU_PITCH = EXT_ROWS // 8
assert LRU_PITCH * 8 == EXT_ROWS and LRU_PITCH % 4 != 0
SUM_ROWS = 16
LOG2_E = 1.4426950408889634
F32_MIN_NORMAL = 1.1754943508222875e-38
VMEM_LIMIT = 56 * 1024 * 1024

F32 = jnp.float32
BF16 = jnp.bfloat16


def _params(sem, vmem=VMEM_LIMIT):
    return pltpu.CompilerParams(dimension_semantics=sem, vmem_limit_bytes=vmem)


def _const_spec(shape):
    n = len(shape)
    return pl.BlockSpec(shape, lambda *_: (0,) * n, pipeline_mode=pl.Buffered(1))


def _mod_kernel(c_ref, w_ref, b_ref, o_ref):
    s = jax.nn.silu(c_ref[...]).astype(BF16)
    o_ref[0] = jnp.dot(s, w_ref[0].astype(BF16), preferred_element_type=F32) + b_ref[0]


def _modulation(cond, w_mod, b_mod):
    depth, d, d3 = w_mod.shape
    rows = cond.shape[0]
    nchunk = d3 // d
    return pl.pallas_call(
        _mod_kernel,
        grid=(depth, nchunk),
        in_specs=[
            pl.BlockSpec((rows, d), lambda l, j: (0, 0)),
            pl.BlockSpec((1, d, d), lambda l, j: (l, 0, j)),
            pl.BlockSpec((1, 1, d), lambda l, j: (l, 0, j)),
        ],
        out_specs=pl.BlockSpec((1, rows, d), lambda l, j: (l, 0, j)),
        out_shape=jax.ShapeDtypeStruct((depth, rows, d3), F32),
        compiler_params=_params(("arbitrary", "arbitrary")),
        name="modulation",
    )(cond, w_mod, b_mod.reshape(depth, 1, d3))


def _head_norm_rope(z, g, cos, sgn):
    h, hd, tm = z.shape
    ms = jnp.mean(z * z, axis=1, keepdims=True)
    y = z * lax.rsqrt(ms + EPS) * g
    quarter = hd // 4
    y4 = y.reshape(h * 2, 2, quarter, tm)
    swapped = jnp.concatenate([y4[:, 1:2], y4[:, 0:1]], axis=1).reshape(h, hd, tm)
    return y * cos + swapped * sgn


def _in_proj_kernel(x_ref, mod_ref, ng_ref, wt_ref, wn_ref, gq_ref, gk_ref, cos_ref, sgn_ref,
                    q_ref, k_ref, v_ref, rest_ref):
    d_attn = N_Q_HEADS * HEAD_DIM
    d_kv = N_KV_HEADS * HEAD_DIM
    x = x_ref[0]
    tm = x.shape[0]
    ms = jnp.mean(x * x, axis=-1, keepdims=True)
    y = x * lax.rsqrt(ms + EPS) * ng_ref[...]
    shift = mod_ref[0, 0, 0:1, :]
    scale = mod_ref[0, 0, 1:2, :]
    h = (y * (1.0 + scale) + shift).astype(BF16)

    pt = lax.dot_general(wt_ref[...], h, (((1,), (1,)), ((), ())), preferred_element_type=F32)
    cos = cos_ref[...]
    sgn = sgn_ref[...]
    q = _head_norm_rope(pt[0:d_attn].reshape(N_Q_HEADS, HEAD_DIM, tm), gq_ref[...], cos, sgn)
    q_ref[0] = (q * (HEAD_DIM ** -0.5 * LOG2_E)).reshape(d_attn, tm).astype(BF16)
    k = _head_norm_rope(pt[d_attn:d_attn + d_kv].reshape(N_KV_HEADS, HEAD_DIM, tm), gk_ref[...], cos, sgn)
    kn = k.reshape(d_kv, tm).T
    for j in range(N_KV_HEADS):
        k_ref[0, j] = kn[:, j * HEAD_DIM:(j + 1) * HEAD_DIM].astype(BF16)
        v_ref[0, j, 0] = pt[d_attn + d_kv + j * HEAD_DIM:d_attn + d_kv + (j + 1) * HEAD_DIM].astype(BF16)

    n_rest = wn_ref.shape[1]
    step = 1024
    for c in range(0, n_rest, step):
        rest_ref[0, :, c:c + step] = jnp.dot(h, wn_ref[:, c:c + step], preferred_element_type=F32)


def _in_proj(xs, modsel, ng, wt, wn, gq, gk, cos_t, sgn_t, n_ctx_tiles):
    b, t, d = xs.shape
    nt = t // TILE
    d_attn = N_Q_HEADS * HEAD_DIM
    n_rest = wn.shape[1]
    return pl.pallas_call(
        _in_proj_kernel,
        grid=(b, nt),
        in_specs=[
            pl.BlockSpec((1, TILE, d), lambda i, j: (i, j, 0)),
            pl.BlockSpec((1, 1, 3, d), lambda i, j: (i, jnp.where(j < n_ctx_tiles, 0, 1), 0, 0)),
            _const_spec((1, d)),
            _const_spec(wt.shape),
            _const_spec(wn.shape),
            _const_spec((HEAD_DIM, 1)),
            _const_spec((HEAD_DIM, 1)),
            pl.BlockSpec((HEAD_DIM, TILE), lambda i, j: (0, j)),
            pl.BlockSpec((HEAD_DIM, TILE), lambda i, j: (0, j)),
        ],
        out_specs=[
            pl.BlockSpec((1, d_attn, TILE), lambda i, j: (i, 0, j)),
            pl.BlockSpec((1, N_KV_HEADS, TILE, HEAD_DIM), lambda i, j: (i, 0, j, 0)),
            pl.BlockSpec((1, N_KV_HEADS, 1, HEAD_DIM, TILE), lambda i, j: (i, 0, j, 0, 0)),
            pl.BlockSpec((1, TILE, n_rest), lambda i, j: (i, j, 0)),
        ],
        out_shape=[
            jax.ShapeDtypeStruct((b, d_attn, t), BF16),
            jax.ShapeDtypeStruct((b, N_KV_HEADS, t, HEAD_DIM), BF16),
            jax.ShapeDtypeStruct((b, N_KV_HEADS, nt, HEAD_DIM, TILE), BF16),
            jax.ShapeDtypeStruct((b, t, n_rest), F32),
        ],
        compiler_params=_params(("parallel", "parallel")),
        name="in_proj",
    )(xs, modsel, ng, wt, wn, gq, gk, cos_t, sgn_t)


def _attn_kernel(q_ref, k_ref, v_ref, ga_ref, o_ref, s_ref, mx_ref, m_ref, acc_ref,
                 *, n_ctx_tiles, n_tiles):
    qi = pl.program_id(2)
    m_ref[...] = jnp.full(m_ref.shape, -jnp.inf, F32)
    acc_ref[...] = jnp.zeros(acc_ref.shape, F32)
    ones_rows = jnp.ones((SUM_ROWS, TILE), BF16)

    def scores(c, slot):
        kc = k_ref[0, 0, c * TILE:(c + 1) * TILE, :]
        for g in range(Q_PER_KV):
            qg = q_ref[0, g * HEAD_DIM:(g + 1) * HEAD_DIM, :]
            s = jnp.dot(kc, qg, preferred_element_type=F32)
            s_ref[slot, g] = s
            mx_ref[slot, g] = jnp.max(s, axis=0, keepdims=True)

    def update(c, slot):
        vc = jnp.concatenate([v_ref[0, 0, c], ones_rows], axis=0)
        for g in range(Q_PER_KV):
            m_prev = m_ref[g]
            m_new = jnp.maximum(m_prev, mx_ref[slot, g])
            alpha = jnp.exp2(m_prev - m_new)
            p = jnp.exp2(s_ref[slot, g] - m_new)
            acc_ref[g] = alpha * acc_ref[g] + jnp.dot(vc, p.astype(BF16), preferred_element_type=F32)
            m_ref[g] = m_new

    def run(n_chunks):
        scores(0, 0)
        for c in range(n_chunks):
            if c + 1 < n_chunks:
                scores(c + 1, (c + 1) % 2)
            update(c, c % 2)
        o = jnp.concatenate([acc_ref[g, 0:HEAD_DIM] / acc_ref[g, HEAD_DIM:HEAD_DIM + 1]
                             for g in range(Q_PER_KV)], axis=0)
        ga = ga_ref[0]
        o_ref[0] = (o.T * (ga * _sigmoid(ga))).astype(BF16)

    @pl.when(qi < n_ctx_tiles)
    def _():
        run(n_ctx_tiles)

    @pl.when(qi >= n_ctx_tiles)
    def _():
        run(n_tiles)


def _attention(qt, k4, vt, rest, ga_col, n_ctx_tiles):
    b, d_attn, t = qt.shape
    nt = t // TILE
    width = Q_PER_KV * HEAD_DIM
    kern = functools.partial(_attn_kernel, n_ctx_tiles=n_ctx_tiles, n_tiles=nt)
    return pl.pallas_call(
        kern,
        grid=(b, N_KV_HEADS, nt),
        in_specs=[
            pl.BlockSpec((1, width, TILE), lambda i, j, q: (i, j, q)),
            pl.BlockSpec((1, 1, t, HEAD_DIM), lambda i, j, q: (i, j, 0, 0)),
            pl.BlockSpec((1, 1, nt, HEAD_DIM, TILE), lambda i, j, q: (i, j, 0, 0, 0)),
            pl.BlockSpec((1, TILE, width), lambda i, j, q: (i, q, ga_col // width + j)),
        ],
        out_specs=pl.BlockSpec((1, TILE, width), lambda i, j, q: (i, q, j)),
        out_shape=jax.ShapeDtypeStruct((b, t, d_attn), BF16),
        scratch_shapes=[
            pltpu.VMEM((2, Q_PER_KV, TILE, TILE), F32),
            pltpu.VMEM((2, Q_PER_KV, 1, TILE), F32),
            pltpu.VMEM((Q_PER_KV, 1, TILE), F32),
            pltpu.VMEM((Q_PER_KV, HEAD_DIM + SUM_ROWS, TILE), F32),
        ],
        compiler_params=_params(("parallel", "parallel", "arbitrary")),
        name="attention",
    )(qt, k4, vt, rest)


def _softplus(x):
    return jnp.maximum(x, 0.0) + jnp.log1p(jnp.exp(-jnp.abs(x)))


def _sigmoid(x):
    return 0.5 + 0.5 * jnp.tanh(0.5 * x)


def _lru_tile_index(i, n_ctx_tiles, n_tiles, reverse):
    if not reverse:
        return i
    return jnp.where(i < n_ctx_tiles, n_ctx_tiles - 1 - i, n_tiles - 1 - (i - n_ctx_tiles))


def _lru_kernel(*refs, n_ctx_tiles, n_tiles, reverse):
    if reverse:
        (u_ref, up_ref, un_ref, cw_ref, cb_ref, wg_ref, bg_ref, lam_ref, hf_ref, gb_ref,
         o_ref, carry_ref, ext_ref, uc_ref, a_ref, b_ref, h_ref) = refs
    else:
        (u_ref, up_ref, un_ref, cw_ref, cb_ref, wg_ref, bg_ref, lam_ref,
         o_ref, carry_ref, ext_ref, uc_ref, a_ref, b_ref, h_ref) = refs
    i = pl.program_id(1)
    tb = _lru_tile_index(i, n_ctx_tiles, n_tiles, reverse)

    @pl.when(i == 0)
    def _():
        carry_ref[...] = jnp.zeros(carry_ref.shape, F32)

    d = u_ref.shape[2]
    n_lane_tiles = d // LANES
    per_group = LRU_GROUP // LANES
    n_sub = EXT_ROWS // LRU_PITCH
    has_prev = jnp.logical_and(tb != 0, tb != n_ctx_tiles)
    has_next = jnp.logical_and(tb != n_ctx_tiles - 1, tb != n_tiles - 1)
    zeros = jnp.zeros((HALO, LANES), F32)
    for lt in range(n_lane_tiles):
        sl = slice(lt * LANES, (lt + 1) * LANES)
        ext_ref[lt, 0:HALO] = zeros
        ext_ref[lt, HALO:2 * HALO] = jnp.where(has_prev, up_ref[0, :, sl], 0.0)
        ext_ref[lt, 2 * HALO:2 * HALO + TILE] = u_ref[0, :, sl]
        ext_ref[lt, 2 * HALO + TILE:3 * HALO + TILE] = jnp.where(has_next, un_ref[0, :, sl], 0.0)
        ext_ref[lt, 3 * HALO + TILE:4 * HALO + TILE] = zeros

    p = lax.broadcasted_iota(jnp.int32, (n_sub * HALO, LRU_GROUP), 0)
    valid_head = LRU_PITCH * (p % n_sub) + p // n_sub >= HALO
    valid_tail = LRU_PITCH * (p % n_sub) + p // n_sub + (LRU_PITCH - HALO) < HALO + TILE

    for cg in range(n_lane_tiles // per_group):
        lo, hi = cg * LRU_GROUP, (cg + 1) * LRU_GROUP
        for hh in range(per_group):
            lt = cg * per_group + hh
            sl = slice(lt * LANES, (lt + 1) * LANES)
            taps = [cw_ref[j:j + 1, sl] for j in range(CONV_W)]
            bias = cb_ref[:, sl]
            for r in range(LRU_PITCH):
                acc = None
                for j in range(CONV_W):
                    x = ext_ref[lt, pl.ds(HALO + r + j - CONV_PAD_LEFT, n_sub, stride=LRU_PITCH), :]
                    acc = x * taps[j] if acc is None else acc + x * taps[j]
                uc_ref[r * n_sub:(r + 1) * n_sub, hh * LANES:(hh + 1) * LANES] = acc + bias
        uc = uc_ref[...]
        pre = jnp.dot(uc.astype(BF16), wg_ref[cg], preferred_element_type=F32)
        tr = jnp.tanh(pre[:, :LRU_GROUP] + 0.5 * bg_ref[0:1, lo:hi])
        ti = jnp.tanh(pre[:, LRU_GROUP:] + 0.5 * bg_ref[1:2, lo:hi])
        half_c = (-0.5 * LRU_C) * _softplus(-lam_ref[:, lo:hi])
        log_a = half_c + half_c * tr
        a = jnp.exp(log_a)
        th = jnp.tanh(log_a)
        zz = (-0.5 * th) / (1.0 - th)
        root = zz * lax.rsqrt(jnp.maximum(zz, F32_MIN_NORMAL))
        bb = (root * uc) * (1.0 + ti)
        head, tail = n_sub * HALO, n_sub * (LRU_PITCH - HALO)
        a_ref[0:head] = jnp.where(valid_head, a[0:head], 1.0)
        b_ref[0:head] = jnp.where(valid_head, bb[0:head], 0.0)
        a_ref[head:tail] = a[head:tail]
        b_ref[head:tail] = bb[head:tail]
        a_ref[tail:] = jnp.where(valid_tail, a[tail:], 1.0)
        b_ref[tail:] = jnp.where(valid_tail, bb[tail:], 0.0)

        for hh in range(per_group):
            lt = cg * per_group + hh
            sl = slice(lt * LANES, (lt + 1) * LANES)
            cols = slice(hh * LANES, (hh + 1) * LANES)
            order = range(LRU_PITCH - 1, -1, -1) if reverse else range(LRU_PITCH)
            h_run = a_run = None
            for r in order:
                rows = slice(r * n_sub, (r + 1) * n_sub)
                a_r = a_ref[rows, cols]
                b_r = b_ref[rows, cols]
                h_run = b_r if h_run is None else a_r * h_run + b_r
                a_run = a_r if a_run is None else a_run * a_r
                b_ref[rows, cols] = h_run
                a_ref[rows, cols] = a_run
            state = carry_ref[:, sl]
            carry_in = [None] * n_sub
            for s in (range(n_sub - 1, -1, -1) if reverse else range(n_sub)):
                carry_in[s] = state
                state = a_run[s:s + 1] * state + h_run[s:s + 1]
            carry_ref[:, sl] = state
            cin = jnp.concatenate(carry_in, axis=0)
            for r in range(LRU_PITCH):
                rows = slice(r * n_sub, (r + 1) * n_sub)
                h_ref[lt, pl.ds(r, n_sub, stride=LRU_PITCH), :] = b_ref[rows, cols] + a_ref[rows, cols] * cin

    for lt in range(n_lane_tiles):
        sl = slice(lt * LANES, (lt + 1) * LANES)
        h = h_ref[lt, HALO:HALO + TILE, :]
        if reverse:
            g = gb_ref[0, :, sl]
            o_ref[0, :, sl] = ((hf_ref[0, :, sl] + h) * (g * _sigmoid(g))).astype(BF16)
        else:
            o_ref[0, :, sl] = h


def _lru(rest, u_col, gb_col, conv_w, conv_b, wg, bg, lam, hf, n_ctx_tiles, reverse):
    b, t, _ = rest.shape
    d = conv_w.shape[1]
    nt = t // TILE
    per_tile = TILE // HALO
    n_halo_blocks = t // HALO
    tile = functools.partial(_lru_tile_index, n_ctx_tiles=n_ctx_tiles, n_tiles=nt, reverse=reverse)
    ucol = u_col // d
    in_specs = [
        pl.BlockSpec((1, TILE, d), lambda i, j: (i, tile(j), ucol)),
        pl.BlockSpec((1, HALO, d), lambda i, j: (i, jnp.maximum(tile(j) * per_tile - 1, 0), ucol)),
        pl.BlockSpec((1, HALO, d),
                     lambda i, j: (i, jnp.minimum((tile(j) + 1) * per_tile, n_halo_blocks - 1), ucol)),
        _const_spec(conv_w.shape),
        _const_spec((1, d)),
        _const_spec(wg.shape),
        _const_spec(bg.shape),
        _const_spec((1, d)),
    ]
    args = [rest, rest, rest, conv_w, conv_b, wg, bg, lam]
    if reverse:
        in_specs += [
            pl.BlockSpec((1, TILE, d), lambda i, j: (i, tile(j), 0)),
            pl.BlockSpec((1, TILE, d), lambda i, j: (i, tile(j), gb_col // d)),
        ]
        args += [hf, rest]
    kern = functools.partial(_lru_kernel, n_ctx_tiles=n_ctx_tiles, n_tiles=nt, reverse=reverse)
    return pl.pallas_call(
        kern,
        grid=(b, nt),
        in_specs=in_specs,
        out_specs=pl.BlockSpec((1, TILE, d), lambda i, j: (i, tile(j), 0)),
        out_shape=jax.ShapeDtypeStruct((b, t, d), BF16 if reverse else F32),
        scratch_shapes=[
            pltpu.VMEM((1, d), F32),
            pltpu.VMEM((d // LANES, EXT_ROWS + 2 * HALO, LANES), F32),
            pltpu.VMEM((EXT_ROWS, LRU_GROUP), F32),
            pltpu.VMEM((EXT_ROWS, LRU_GROUP), F32),
            pltpu.VMEM((EXT_ROWS, LRU_GROUP), F32),
            pltpu.VMEM((d // LANES, EXT_ROWS, LANES), F32),
        ],
        compiler_params=_params(("parallel", "arbitrary")),
        name="lru_rev" if reverse else "lru_fwd",
    )(*args)


def _merge_kernel(oa_ref, ob_ref, gm_ref, x_ref, mod_ref, wa_ref, wb_ref, wo_ref, o_ref):
    d = x_ref.shape[2]
    ya = jnp.dot(oa_ref[0], wa_ref[...], preferred_element_type=F32)
    yb = jnp.dot(ob_ref[0], wb_ref[...], preferred_element_type=F32)
    mix = _sigmoid(gm_ref[0, :, 0:d]) * ya + _sigmoid(gm_ref[0, :, d:2 * d]) * yb
    out = jnp.dot(mix.astype(BF16), wo_ref[...], preferred_element_type=F32)
    o_ref[0] = x_ref[0] + mod_ref[0, 0, 2:3, :] * out


def _merge(oa, ob, rest, xs, modsel, wa, wb, wo, n_ctx_tiles, latent_only):
    b, t, d = xs.shape
    nt = t // TILE
    skip = n_ctx_tiles if latent_only else 0
    return pl.pallas_call(
        _merge_kernel,
        grid=(b, nt - skip),
        in_specs=[
            pl.BlockSpec((1, TILE, d), lambda i, j: (i, j + skip, 0)),
            pl.BlockSpec((1, TILE, d), lambda i, j: (i, j + skip, 0)),
            pl.BlockSpec((1, TILE, 2 * d), lambda i, j: (i, j + skip, 0)),
            pl.BlockSpec((1, TILE, d), lambda i, j: (i, j + skip, 0)),
            pl.BlockSpec((1, 1, 3, d), lambda i, j: (i, jnp.where(j + skip < n_ctx_tiles, 0, 1), 0, 0)),
            _const_spec(wa.shape),
            _const_spec(wb.shape),
            _const_spec(wo.shape),
        ],
        out_specs=pl.BlockSpec((1, TILE, d), lambda i, j: (i, j, 0)),
        out_shape=jax.ShapeDtypeStruct((b, t - skip * TILE, d), F32),
        compiler_params=_params(("parallel", "parallel")),
        name="merge",
    )(oa, ob, rest, xs, modsel, wa, wb, wo)


def _rope_tables(n, n_ctx):
    quarter = HEAD_DIM // 4
    freqs = ROPE_THETA ** (-jnp.arange(quarter, dtype=F32) / quarter)
    pos = jnp.arange(n)
    ang_r = (pos // GRID_W).astype(F32)[None, :] * freqs[:, None]
    ang_c = (pos % GRID_W).astype(F32)[None, :] * freqs[:, None]
    cos = jnp.concatenate([jnp.cos(ang_r)] * 2 + [jnp.cos(ang_c)] * 2, axis=0)
    sgn = jnp.concatenate([-jnp.sin(ang_r), jnp.sin(ang_r), -jnp.sin(ang_c), jnp.sin(ang_c)], axis=0)
    cos = jnp.concatenate([jnp.ones((HEAD_DIM, n_ctx), F32), cos], axis=1)
    sgn = jnp.concatenate([jnp.zeros((HEAD_DIM, n_ctx), F32), sgn], axis=1)
    return cos, sgn


def _gate_weights(w):
    per = LRU_GROUP // (w.shape[-1])
    n_groups = LRU_BLOCKS // per
    blk = w.shape[-1]
    wg = w.reshape(2, n_groups, per, blk, blk)
    eye = jnp.eye(per, dtype=w.dtype)
    dense = jnp.einsum('gnpij,pq->gnpiqj', wg, eye).reshape(2, n_groups, per * blk, per * blk)
    return (0.5 * jnp.concatenate([dense[0], dense[1]], axis=-1)).astype(BF16)


def kernel(x, c, ctx, c_ctx, norm_g, w_mod, b_mod, w_in, q_norm_g, k_norm_g, conv_w, conv_b,
           lru_gate_w, lru_gate_b, lru_lambda, w_a_out, w_b_out, w_out):
    bsz, n, d = x.shape
    n_ctx = ctx.shape[1]
    depth = w_in.shape[0]
    assert n % TILE == 0 and n_ctx % TILE == 0 and n % GRID_W == 0
    d_attn = N_Q_HEADS * HEAD_DIM
    d_kv = N_KV_HEADS * HEAD_DIM
    d_lru = conv_w.shape[-1]
    assert d_lru == d and d_attn == d and w_in.shape[2] == 2 * d_attn + 2 * d_kv + 2 * d_lru + 2 * d
    n_ctx_tiles = n_ctx // TILE

    xs = jnp.concatenate([ctx, x], axis=1)
    rows = -(-(bsz + 1) // 8) * 8
    cond = jnp.zeros((rows, d), F32).at[:bsz].set(c).at[bsz].set(c_ctx)
    mod = _modulation(cond, w_mod, b_mod)
    cos_t, sgn_t = _rope_tables(n, n_ctx)

    o_q, o_k, o_v = 0, d_attn, d_attn + d_kv
    o_ga = d_attn + 2 * d_kv
    o_u = o_ga + d_attn
    o_gb = o_u + d_lru
    o_gm = o_gb + d_lru
    ga_col, u_col, gb_col = 2 * d, 2 * d + d_attn, 2 * d + d_attn + d_lru

    for l in range(depth):
        m3 = mod[l].reshape(rows, 3, d)
        modsel = jnp.stack([jnp.broadcast_to(m3[bsz], (bsz, 3, d)), m3[:bsz]], axis=1)
        w = w_in[l]
        wt = w[:, o_q:o_ga].T.astype(BF16)
        wn = jnp.concatenate([w[:, o_gm:], w[:, o_ga:o_u], w[:, o_u:o_gb], w[:, o_gb:o_gm]],
                             axis=1).astype(BF16)
        qt, k4, vt, rest = _in_proj(xs, modsel, norm_g[l][None, :], wt, wn,
                                    q_norm_g[l][:, None], k_norm_g[l][:, None], cos_t, sgn_t, n_ctx_tiles)
        oa = _attention(qt, k4, vt, rest, ga_col, n_ctx_tiles)
        hf = None
        for dr, rev in enumerate((False, True)):
            hf = _lru(rest, u_col, gb_col, conv_w[l], conv_b[l][None, :], _gate_weights(lru_gate_w[l, dr]),
                      lru_gate_b[l, dr], lru_lambda[l, dr][None, :], hf, n_ctx_tiles, rev)
        xs = _merge(oa, hf, rest, xs, modsel, w_a_out[l].astype(BF16), w_b_out[l].astype(BF16),
                    w_out[l].astype(BF16), n_ctx_tiles, latent_only=(l == depth - 1))
    return xs
```

```python
import functools
import math

import jax
import jax.numpy as jnp
from jax import lax
from jax.experimental import pallas as pl
from jax.experimental.pallas import tpu as pltpu

GRID_W = 64
HEAD_DIM = 64
N_Q_HEADS = 16
N_KV_HEADS = 4
Q_PER_KV = N_Q_HEADS // N_KV_HEADS
LRU_BLOCKS = 16
CONV_W = 4
CONV_PAD_LEFT = 2
LRU_C = 8.0
ROPE_THETA = 10000.0
EPS = 1e-6

TILE = 256
HALO = 8
LANES = 128
LRU_GROUP = 256
EXT_ROWS = TILE + 2 * HALO
LRU_PITCH = EXT_ROWS // 8
assert LRU_PITCH * 8 == EXT_ROWS and LRU_PITCH % 4 != 0
IN_PROJ_BATCH = 2
MERGE_BATCH = 2
LRU_BATCH = 4
SUM_ROWS = 16
LOG2_E = 1.4426950408889634
F32_MIN_NORMAL = 1.1754943508222875e-38
VMEM_LIMIT = 56 * 1024 * 1024

F32 = jnp.float32
BF16 = jnp.bfloat16


def _params(sem, vmem=VMEM_LIMIT):
    return pltpu.CompilerParams(dimension_semantics=sem, vmem_limit_bytes=vmem)


def _const_spec(shape):
    n = len(shape)
    return pl.BlockSpec(shape, lambda *_: (0,) * n, pipeline_mode=pl.Buffered(1))


def _mod_kernel(c_ref, w_ref, b_ref, o_ref):
    s = jax.nn.silu(c_ref[...]).astype(BF16)
    o_ref[0] = jnp.dot(s, w_ref[0].astype(BF16), preferred_element_type=F32) + b_ref[0]


def _modulation(cond, w_mod, b_mod):
    depth, d, d3 = w_mod.shape
    rows = cond.shape[0]
    nchunk = d3 // d
    return pl.pallas_call(
        _mod_kernel,
        grid=(depth, nchunk),
        in_specs=[
            pl.BlockSpec((rows, d), lambda l, j: (0, 0)),
            pl.BlockSpec((1, d, d), lambda l, j: (l, 0, j)),
            pl.BlockSpec((1, 1, d), lambda l, j: (l, 0, j)),
        ],
        out_specs=pl.BlockSpec((1, rows, d), lambda l, j: (l, 0, j)),
        out_shape=jax.ShapeDtypeStruct((depth, rows, d3), F32),
        compiler_params=_params(("arbitrary", "arbitrary")),
        name="modulation",
    )(cond, w_mod, b_mod.reshape(depth, 1, d3))


def _head_norm_rope(z, g, cos, sgn):
    h, hd, tm = z.shape
    ms = jnp.mean(z * z, axis=1, keepdims=True)
    y = z * lax.rsqrt(ms + EPS) * g
    quarter = hd // 4
    y4 = y.reshape(h * 2, 2, quarter, tm)
    swapped = jnp.concatenate([y4[:, 1:2], y4[:, 0:1]], axis=1).reshape(h, hd, tm)
    return y * cos + swapped * sgn


def _in_proj_kernel(x_ref, mod_ref, ng_ref, wt_ref, wn_ref, gq_ref, gk_ref, cos_ref, sgn_ref,
                    q_ref, k_ref, v_ref, rest_ref):
    d_attn = N_Q_HEADS * HEAD_DIM
    d_kv = N_KV_HEADS * HEAD_DIM
    width = Q_PER_KV * HEAD_DIM
    tm = x_ref.shape[1]
    n_rest = wn_ref.shape[1]

    def one(bi, carry):
        x = x_ref[bi]
        ms = jnp.mean(x * x, axis=-1, keepdims=True)
        y = x * lax.rsqrt(ms + EPS) * ng_ref[...]
        shift = mod_ref[bi, 0, 0:1, :]
        scale = mod_ref[bi, 0, 1:2, :]
        h = (y * (1.0 + scale) + shift).astype(BF16)

        pt = lax.dot_general(wt_ref[...], h, (((1,), (1,)), ((), ())), preferred_element_type=F32)
        cos = cos_ref[...]
        sgn = sgn_ref[...]
        q = _head_norm_rope(pt[0:d_attn].reshape(N_Q_HEADS, HEAD_DIM, tm), gq_ref[...], cos, sgn)
        qs = (q * (HEAD_DIM ** -0.5 * LOG2_E)).reshape(d_attn, tm).astype(BF16)
        for j in range(N_KV_HEADS):
            q_ref[bi, j, 0] = qs[j * width:(j + 1) * width]
        k = _head_norm_rope(pt[d_attn:d_attn + d_kv].reshape(N_KV_HEADS, HEAD_DIM, tm), gk_ref[...], cos, sgn)
        kn = k.reshape(d_kv, tm).T
        for j in range(N_KV_HEADS):
            k_ref[bi, j] = kn[:, j * HEAD_DIM:(j + 1) * HEAD_DIM].astype(BF16)
            v_ref[bi, j, 0] = pt[d_attn + d_kv + j * HEAD_DIM:d_attn + d_kv + (j + 1) * HEAD_DIM].astype(BF16)

        step = 1024
        for c in range(0, n_rest, step):
            rest_ref[bi, :, c:c + step] = jnp.dot(h, wn_ref[:, c:c + step], preferred_element_type=F32)
        return carry

    lax.fori_loop(0, x_ref.shape[0], one, 0)


def _in_proj(xs, modsel, ng, wt, wn, gq, gk, cos_t, sgn_t, n_ctx_tiles):
    b, t, d = xs.shape
    nt = t // TILE
    d_attn = N_Q_HEADS * HEAD_DIM
    n_rest = wn.shape[1]
    nb = math.gcd(b, IN_PROJ_BATCH)
    return pl.pallas_call(
        _in_proj_kernel,
        grid=(b // nb, nt),
        in_specs=[
            pl.BlockSpec((nb, TILE, d), lambda i, j: (i, j, 0)),
            pl.BlockSpec((nb, 1, 3, d), lambda i, j: (i, jnp.where(j < n_ctx_tiles, 0, 1), 0, 0)),
            _const_spec((1, d)),
            _const_spec(wt.shape),
            _const_spec(wn.shape),
            _const_spec((HEAD_DIM, 1)),
            _const_spec((HEAD_DIM, 1)),
            pl.BlockSpec((HEAD_DIM, TILE), lambda i, j: (0, j)),
            pl.BlockSpec((HEAD_DIM, TILE), lambda i, j: (0, j)),
        ],
        out_specs=[
            pl.BlockSpec((nb, N_KV_HEADS, 1, d_attn // N_KV_HEADS, TILE), lambda i, j: (i, 0, j, 0, 0)),
            pl.BlockSpec((nb, N_KV_HEADS, TILE, HEAD_DIM), lambda i, j: (i, 0, j, 0)),
            pl.BlockSpec((nb, N_KV_HEADS, 1, HEAD_DIM, TILE), lambda i, j: (i, 0, j, 0, 0)),
            pl.BlockSpec((nb, TILE, n_rest), lambda i, j: (i, j, 0)),
        ],
        out_shape=[
            jax.ShapeDtypeStruct((b, N_KV_HEADS, nt, d_attn // N_KV_HEADS, TILE), BF16),
            jax.ShapeDtypeStruct((b, N_KV_HEADS, t, HEAD_DIM), BF16),
            jax.ShapeDtypeStruct((b, N_KV_HEADS, nt, HEAD_DIM, TILE), BF16),
            jax.ShapeDtypeStruct((b, t, n_rest), F32),
        ],
        compiler_params=_params(("parallel", "parallel")),
        name="in_proj",
    )(xs, modsel, ng, wt, wn, gq, gk, cos_t, sgn_t)


def _attn_kernel(q_ref, k_ref, v_ref, ga_ref, o_ref, s_ref, mx_ref, m_ref, acc_ref,
                 *, n_ctx_tiles, n_tiles):
    ones_rows = jnp.ones((SUM_ROWS, TILE), BF16)

    def scores(qt, c, slot):
        kc = k_ref[0, 0, c * TILE:(c + 1) * TILE, :]
        for g in range(Q_PER_KV):
            qg = q_ref[0, 0, qt, g * HEAD_DIM:(g + 1) * HEAD_DIM, :]
            s = jnp.dot(kc, qg, preferred_element_type=F32)
            s_ref[slot, g] = s
            mx_ref[slot, g] = jnp.max(s, axis=0, keepdims=True)

    def update(c, slot, acc_set, first):
        vc = jnp.concatenate([v_ref[0, 0, c], ones_rows], axis=0)
        for g in range(Q_PER_KV):
            if first:
                m_new = mx_ref[slot, g]
                p = jnp.exp2(s_ref[slot, g] - m_new)
                acc_ref[acc_set, g] = jnp.dot(vc, p.astype(BF16), preferred_element_type=F32)
            else:
                m_prev = m_ref[acc_set, g]
                m_new = jnp.maximum(m_prev, mx_ref[slot, g])
                alpha = jnp.exp2(m_prev - m_new)
                p = jnp.exp2(s_ref[slot, g] - m_new)
                acc_ref[acc_set, g] = (alpha * acc_ref[acc_set, g]
                                       + jnp.dot(vc, p.astype(BF16), preferred_element_type=F32))
            m_ref[acc_set, g] = m_new

    def finish(qt, acc_set):
        o = jnp.concatenate([acc_ref[acc_set, g, 0:HEAD_DIM] / acc_ref[acc_set, g, HEAD_DIM:HEAD_DIM + 1]
                             for g in range(Q_PER_KV)], axis=0)
        if isinstance(qt, int):
            rows = slice(qt * TILE, (qt + 1) * TILE)
        else:
            rows = pl.ds(pl.multiple_of(qt * TILE, TILE), TILE)
        ga = ga_ref[0, rows, :]
        o_ref[0, rows, :] = (o.T * (ga * _sigmoid(ga))).astype(BF16)

    def tile(qt, parity, chunks, next_qt, next_chunk):
        for idx, c in enumerate(chunks):
            slot = (parity + idx) % 2
            if idx + 1 < len(chunks):
                scores(qt, chunks[idx + 1], 1 - slot)
            else:
                scores(next_qt, next_chunk, 1 - slot)
            update(c, slot, parity, idx == 0)
        finish(qt, parity)

    ctx_chunks = list(range(n_ctx_tiles))
    all_chunks = list(range(n_tiles))
    parity = 0
    scores(0, 0, parity)
    for qt in range(n_ctx_tiles):
        tile(qt, parity, ctx_chunks, qt + 1, 0)
        parity = (parity + len(ctx_chunks)) % 2
    n_latent = n_tiles - n_ctx_tiles
    pairs = n_latent // 2
    odd = len(all_chunks) % 2
    first_parity = parity

    def pair(i, carry):
        qa = n_ctx_tiles + 2 * i
        tile(qa, first_parity, all_chunks, qa + 1, 0)
        tile(qa + 1, (first_parity + odd) % 2, all_chunks, jnp.minimum(qa + 2, n_tiles - 1), 0)
        return carry

    lax.fori_loop(0, pairs, pair, 0)


def _attention(q5, k4, vt, rest, ga_col, n_ctx_tiles):
    b, _, nt, width, _ = q5.shape
    t = nt * TILE
    d_attn = N_KV_HEADS * width
    assert (nt - n_ctx_tiles) % 2 == 0
    kern = functools.partial(_attn_kernel, n_ctx_tiles=n_ctx_tiles, n_tiles=nt)
    return pl.pallas_call(
        kern,
        grid=(b, N_KV_HEADS),
        in_specs=[
            pl.BlockSpec((1, 1, nt, width, TILE), lambda i, j: (i, j, 0, 0, 0)),
            pl.BlockSpec((1, 1, t, HEAD_DIM), lambda i, j: (i, j, 0, 0)),
            pl.BlockSpec((1, 1, nt, HEAD_DIM, TILE), lambda i, j: (i, j, 0, 0, 0)),
            pl.BlockSpec((1, t, width), lambda i, j: (i, 0, ga_col // width + j)),
        ],
        out_specs=pl.BlockSpec((1, t, width), lambda i, j: (i, 0, j)),
        out_shape=jax.ShapeDtypeStruct((b, t, d_attn), BF16),
        scratch_shapes=[
            pltpu.VMEM((2, Q_PER_KV, TILE, TILE), F32),
            pltpu.VMEM((2, Q_PER_KV, 1, TILE), F32),
            pltpu.VMEM((2, Q_PER_KV, 1, TILE), F32),
            pltpu.VMEM((2, Q_PER_KV, HEAD_DIM + SUM_ROWS, TILE), F32),
        ],
        compiler_params=_params(("parallel", "parallel")),
        name="attention",
    )(q5, k4, vt, rest)


def _softplus(x):
    return jnp.maximum(x, 0.0) + jnp.log1p(jnp.exp(-jnp.abs(x)))


def _sigmoid(x):
    return 0.5 + 0.5 * jnp.tanh(0.5 * x)


def _lru_tile_index(i, n_ctx_tiles, n_tiles, reverse):
    if not reverse:
        return i
    return jnp.where(i < n_ctx_tiles, n_ctx_tiles - 1 - i, n_tiles - 1 - (i - n_ctx_tiles))


def _lru_kernel(*refs, n_ctx_tiles, n_tiles, reverse):
    carry_ref = refs[11 if reverse else 9]

    @pl.when(pl.program_id(1) == 0)
    def _():
        carry_ref[...] = jnp.zeros(carry_ref.shape, F32)

    def one(bi, carry):
        _lru_tile(bi, refs, n_ctx_tiles, n_tiles, reverse)
        return carry

    lax.fori_loop(0, refs[0].shape[0], one, 0)


def _lru_tile(bi, refs, n_ctx_tiles, n_tiles, reverse):
    if reverse:
        (u_ref, up_ref, un_ref, cw_ref, cb_ref, wg_ref, bg_ref, lam_ref, hf_ref, gb_ref,
         o_ref, carry_ref, ext_ref, uc_ref, a_ref, b_ref, h_ref) = refs
    else:
        (u_ref, up_ref, un_ref, cw_ref, cb_ref, wg_ref, bg_ref, lam_ref,
         o_ref, carry_ref, ext_ref, uc_ref, a_ref, b_ref, h_ref) = refs
    tb = _lru_tile_index(pl.program_id(1), n_ctx_tiles, n_tiles, reverse)
    d = u_ref.shape[2]
    n_lane_tiles = d // LANES
    per_group = LRU_GROUP // LANES
    n_sub = EXT_ROWS // LRU_PITCH
    has_prev = jnp.logical_and(tb != 0, tb != n_ctx_tiles)
    has_next = jnp.logical_and(tb != n_ctx_tiles - 1, tb != n_tiles - 1)
    zeros = jnp.zeros((HALO, LANES), F32)
    for lt in range(n_lane_tiles):
        sl = slice(lt * LANES, (lt + 1) * LANES)
        ext_ref[lt, 0:HALO] = zeros
        ext_ref[lt, HALO:2 * HALO] = jnp.where(has_prev, up_ref[bi, :, sl], 0.0)
        ext_ref[lt, 2 * HALO:2 * HALO + TILE] = u_ref[bi, :, sl]
        ext_ref[lt, 2 * HALO + TILE:3 * HALO + TILE] = jnp.where(has_next, un_ref[bi, :, sl], 0.0)
        ext_ref[lt, 3 * HALO + TILE:4 * HALO + TILE] = zeros

    p = lax.broadcasted_iota(jnp.int32, (n_sub * HALO, LRU_GROUP), 0)
    valid_head = LRU_PITCH * (p % n_sub) + p // n_sub >= HALO
    valid_tail = LRU_PITCH * (p % n_sub) + p // n_sub + (LRU_PITCH - HALO) < HALO + TILE

    for cg in range(n_lane_tiles // per_group):
        lo, hi = cg * LRU_GROUP, (cg + 1) * LRU_GROUP
        for hh in range(per_group):
            lt = cg * per_group + hh
            sl = slice(lt * LANES, (lt + 1) * LANES)
            taps = [cw_ref[j:j + 1, sl] for j in range(CONV_W)]
            bias = cb_ref[:, sl]
            for r in range(LRU_PITCH):
                acc = None
                for j in range(CONV_W):
                    x = ext_ref[lt, pl.ds(HALO + r + j - CONV_PAD_LEFT, n_sub, stride=LRU_PITCH), :]
                    acc = x * taps[j] if acc is None else acc + x * taps[j]
                uc_ref[r * n_sub:(r + 1) * n_sub, hh * LANES:(hh + 1) * LANES] = acc + bias
        uc = uc_ref[...]
        pre = jnp.dot(uc.astype(BF16), wg_ref[cg], preferred_element_type=F32)
        tr = jnp.tanh(pre[:, :LRU_GROUP] + 0.5 * bg_ref[0:1, lo:hi])
        ti = jnp.tanh(pre[:, LRU_GROUP:] + 0.5 * bg_ref[1:2, lo:hi])
        half_c = (-0.5 * LRU_C) * _softplus(-lam_ref[:, lo:hi])
        log_a = half_c + half_c * tr
        a = jnp.exp(log_a)
        th = jnp.tanh(log_a)
        zz = (-0.5 * th) / (1.0 - th)
        root = zz * lax.rsqrt(jnp.maximum(zz, F32_MIN_NORMAL))
        bb = (root * uc) * (1.0 + ti)
        head, tail = n_sub * HALO, n_sub * (LRU_PITCH - HALO)
        a_ref[0:head] = jnp.where(valid_head, a[0:head], 1.0)
        b_ref[0:head] = jnp.where(valid_head, bb[0:head], 0.0)
        a_ref[head:tail] = a[head:tail]
        b_ref[head:tail] = bb[head:tail]
        a_ref[tail:] = jnp.where(valid_tail, a[tail:], 1.0)
        b_ref[tail:] = jnp.where(valid_tail, bb[tail:], 0.0)

        for hh in range(per_group):
            lt = cg * per_group + hh
            sl = slice(lt * LANES, (lt + 1) * LANES)
            cols = slice(hh * LANES, (hh + 1) * LANES)
            order = range(LRU_PITCH - 1, -1, -1) if reverse else range(LRU_PITCH)
            h_run = a_run = None
            for r in order:
                rows = slice(r * n_sub, (r + 1) * n_sub)
                a_r = a_ref[rows, cols]
                b_r = b_ref[rows, cols]
                h_run = b_r if h_run is None else a_r * h_run + b_r
                a_run = a_r if a_run is None else a_run * a_r
                b_ref[rows, cols] = h_run
                a_ref[rows, cols] = a_run
            state = carry_ref[bi, :, sl]
            carry_in = [None] * n_sub
            for s in (range(n_sub - 1, -1, -1) if reverse else range(n_sub)):
                carry_in[s] = state
                state = a_run[s:s + 1] * state + h_run[s:s + 1]
            carry_ref[bi, :, sl] = state
            cin = jnp.concatenate(carry_in, axis=0)
            for r in range(LRU_PITCH):
                rows = slice(r * n_sub, (r + 1) * n_sub)
                h_ref[lt, pl.ds(r, n_sub, stride=LRU_PITCH), :] = b_ref[rows, cols] + a_ref[rows, cols] * cin

    for lt in range(n_lane_tiles):
        sl = slice(lt * LANES, (lt + 1) * LANES)
        h = h_ref[lt, HALO:HALO + TILE, :]
        if reverse:
            g = gb_ref[bi, :, sl]
            o_ref[bi, :, sl] = ((hf_ref[bi, :, sl] + h) * (g * _sigmoid(g))).astype(BF16)
        else:
            o_ref[bi, :, sl] = h


def _lru(rest, u_col, gb_col, conv_w, conv_b, wg, bg, lam, hf, n_ctx_tiles, reverse):
    b, t, _ = rest.shape
    d = conv_w.shape[1]
    nt = t // TILE
    per_tile = TILE // HALO
    n_halo_blocks = t // HALO
    tile = functools.partial(_lru_tile_index, n_ctx_tiles=n_ctx_tiles, n_tiles=nt, reverse=reverse)
    ucol = u_col // d
    nb = math.gcd(b, LRU_BATCH)
    in_specs = [
        pl.BlockSpec((nb, TILE, d), lambda i, j: (i, tile(j), ucol)),
        pl.BlockSpec((nb, HALO, d), lambda i, j: (i, jnp.maximum(tile(j) * per_tile - 1, 0), ucol)),
        pl.BlockSpec((nb, HALO, d),
                     lambda i, j: (i, jnp.minimum((tile(j) + 1) * per_tile, n_halo_blocks - 1), ucol)),
        _const_spec(conv_w.shape),
        _const_spec((1, d)),
        _const_spec(wg.shape),
        _const_spec(bg.shape),
        _const_spec((1, d)),
    ]
    args = [rest, rest, rest, conv_w, conv_b, wg, bg, lam]
    if reverse:
        in_specs += [
            pl.BlockSpec((nb, TILE, d), lambda i, j: (i, tile(j), 0)),
            pl.BlockSpec((nb, TILE, d), lambda i, j: (i, tile(j), gb_col // d)),
        ]
        args += [hf, rest]
    kern = functools.partial(_lru_kernel, n_ctx_tiles=n_ctx_tiles, n_tiles=nt, reverse=reverse)
    return pl.pallas_call(
        kern,
        grid=(b // nb, nt),
        in_specs=in_specs,
        out_specs=pl.BlockSpec((nb, TILE, d), lambda i, j: (i, tile(j), 0)),
        out_shape=jax.ShapeDtypeStruct((b, t, d), BF16 if reverse else F32),
        scratch_shapes=[
            pltpu.VMEM((nb, 1, d), F32),
            pltpu.VMEM((d // LANES, EXT_ROWS + 2 * HALO, LANES), F32),
            pltpu.VMEM((EXT_ROWS, LRU_GROUP), F32),
            pltpu.VMEM((EXT_ROWS, LRU_GROUP), F32),
            pltpu.VMEM((EXT_ROWS, LRU_GROUP), F32),
            pltpu.VMEM((d // LANES, EXT_ROWS, LANES), F32),
        ],
        compiler_params=_params(("parallel", "arbitrary")),
        name="lru_rev" if reverse else "lru_fwd",
    )(*args)


def _merge_kernel(oa_ref, ob_ref, gm_ref, x_ref, mod_ref, wa_ref, wb_ref, wo_ref, o_ref):
    d = x_ref.shape[2]

    def one(bi, carry):
        ya = jnp.dot(oa_ref[bi], wa_ref[...], preferred_element_type=F32)
        yb = jnp.dot(ob_ref[bi], wb_ref[...], preferred_element_type=F32)
        mix = _sigmoid(gm_ref[bi, :, 0:d]) * ya + _sigmoid(gm_ref[bi, :, d:2 * d]) * yb
        out = jnp.dot(mix.astype(BF16), wo_ref[...], preferred_element_type=F32)
        o_ref[bi] = x_ref[bi] + mod_ref[bi, 0, 2:3, :] * out
        return carry

    lax.fori_loop(0, x_ref.shape[0], one, 0)


def _merge(oa, ob, rest, xs, modsel, wa, wb, wo, n_ctx_tiles, latent_only):
    b, t, d = xs.shape
    nt = t // TILE
    skip = n_ctx_tiles if latent_only else 0
    nb = math.gcd(b, MERGE_BATCH)
    return pl.pallas_call(
        _merge_kernel,
        grid=(b // nb, nt - skip),
        in_specs=[
            pl.BlockSpec((nb, TILE, d), lambda i, j: (i, j + skip, 0)),
            pl.BlockSpec((nb, TILE, d), lambda i, j: (i, j + skip, 0)),
            pl.BlockSpec((nb, TILE, 2 * d), lambda i, j: (i, j + skip, 0)),
            pl.BlockSpec((nb, TILE, d), lambda i, j: (i, j + skip, 0)),
            pl.BlockSpec((nb, 1, 3, d), lambda i, j: (i, jnp.where(j + skip < n_ctx_tiles, 0, 1), 0, 0)),
            _const_spec(wa.shape),
            _const_spec(wb.shape),
            _const_spec(wo.shape),
        ],
        out_specs=pl.BlockSpec((nb, TILE, d), lambda i, j: (i, j, 0)),
        out_shape=jax.ShapeDtypeStruct((b, t - skip * TILE, d), F32),
        compiler_params=_params(("parallel", "parallel")),
        name="merge",
    )(oa, ob, rest, xs, modsel, wa, wb, wo)


def _rope_tables(n, n_ctx):
    quarter = HEAD_DIM // 4
    freqs = ROPE_THETA ** (-jnp.arange(quarter, dtype=F32) / quarter)
    pos = jnp.arange(n)
    ang_r = (pos // GRID_W).astype(F32)[None, :] * freqs[:, None]
    ang_c = (pos % GRID_W).astype(F32)[None, :] * freqs[:, None]
    cos = jnp.concatenate([jnp.cos(ang_r)] * 2 + [jnp.cos(ang_c)] * 2, axis=0)
    sgn = jnp.concatenate([-jnp.sin(ang_r), jnp.sin(ang_r), -jnp.sin(ang_c), jnp.sin(ang_c)], axis=0)
    cos = jnp.concatenate([jnp.ones((HEAD_DIM, n_ctx), F32), cos], axis=1)
    sgn = jnp.concatenate([jnp.zeros((HEAD_DIM, n_ctx), F32), sgn], axis=1)
    return cos, sgn


def _gate_weights(w):
    per = LRU_GROUP // (w.shape[-1])
    n_groups = LRU_BLOCKS // per
    blk = w.shape[-1]
    wg = w.reshape(2, n_groups, per, blk, blk)
    eye = jnp.eye(per, dtype=w.dtype)
    dense = jnp.einsum('gnpij,pq->gnpiqj', wg, eye).reshape(2, n_groups, per * blk, per * blk)
    return (0.5 * jnp.concatenate([dense[0], dense[1]], axis=-1)).astype(BF16)


def kernel(x, c, ctx, c_ctx, norm_g, w_mod, b_mod, w_in, q_norm_g, k_norm_g, conv_w, conv_b,
           lru_gate_w, lru_gate_b, lru_lambda, w_a_out, w_b_out, w_out):
    bsz, n, d = x.shape
    n_ctx = ctx.shape[1]
    depth = w_in.shape[0]
    assert n % TILE == 0 and n_ctx % TILE == 0 and n % GRID_W == 0
    d_attn = N_Q_HEADS * HEAD_DIM
    d_kv = N_KV_HEADS * HEAD_DIM
    d_lru = conv_w.shape[-1]
    assert d_lru == d and d_attn == d and w_in.shape[2] == 2 * d_attn + 2 * d_kv + 2 * d_lru + 2 * d
    n_ctx_tiles = n_ctx // TILE

    xs = jnp.concatenate([ctx, x], axis=1)
    rows = -(-(bsz + 1) // 8) * 8
    cond = jnp.zeros((rows, d), F32).at[:bsz].set(c).at[bsz].set(c_ctx)
    mod = _modulation(cond, w_mod, b_mod)
    cos_t, sgn_t = _rope_tables(n, n_ctx)

    o_q, o_k, o_v = 0, d_attn, d_attn + d_kv
    o_ga = d_attn + 2 * d_kv
    o_u = o_ga + d_attn
    o_gb = o_u + d_lru
    o_gm = o_gb + d_lru
    ga_col, u_col, gb_col = 2 * d, 2 * d + d_attn, 2 * d + d_attn + d_lru

    for l in range(depth):
        m3 = mod[l].reshape(rows, 3, d)
        modsel = jnp.stack([jnp.broadcast_to(m3[bsz], (bsz, 3, d)), m3[:bsz]], axis=1)
        w = w_in[l]
        wt = w[:, o_q:o_ga].T.astype(BF16)
        wn = jnp.concatenate([w[:, o_gm:], w[:, o_ga:o_u], w[:, o_u:o_gb], w[:, o_gb:o_gm]],
                             axis=1).astype(BF16)
        qt, k4, vt, rest = _in_proj(xs, modsel, norm_g[l][None, :], wt, wn,
                                    q_norm_g[l][:, None], k_norm_g[l][:, None], cos_t, sgn_t, n_ctx_tiles)
        oa = _attention(qt, k4, vt, rest, ga_col, n_ctx_tiles)
        hf = None
        for dr, rev in enumerate((False, True)):
            hf = _lru(rest, u_col, gb_col, conv_w[l], conv_b[l][None, :], _gate_weights(lru_gate_w[l, dr]),
                      lru_gate_b[l, dr], lru_lambda[l, dr][None, :], hf, n_ctx_tiles, rev)
        xs = _merge(oa, hf, rest, xs, modsel, w_a_out[l].astype(BF16), w_b_out[l].astype(BF16),
                    w_out[l].astype(BF16), n_ctx_tiles, latent_only=(l == depth - 1))
    return xs
```

```python
import functools
import math

import jax
import jax.numpy as jnp
from jax import lax
from jax.experimental import pallas as pl
from jax.experimental.pallas import tpu as pltpu

GRID_W = 64
HEAD_DIM = 64
N_Q_HEADS = 16
N_KV_HEADS = 4
Q_PER_KV = N_Q_HEADS // N_KV_HEADS
LRU_BLOCKS = 16
CONV_W = 4
CONV_PAD_LEFT = 2
LRU_C = 8.0
ROPE_THETA = 10000.0
EPS = 1e-6

TILE = 256
HALO = 8
LANES = 128
LRU_GROUP = 256
EXT_ROWS = TILE + 2 * HALO
LRU_PITCH = EXT_ROWS // 8
assert LRU_PITCH * 8 == EXT_ROWS and LRU_PITCH % 4 != 0
IN_PROJ_BATCH = 2
MERGE_BATCH = 2
LRU_BATCH = 4
SUM_ROWS = 16
LOG2_E = 1.4426950408889634
F32_MIN_NORMAL = 1.1754943508222875e-38
VMEM_LIMIT = 56 * 1024 * 1024

F32 = jnp.float32
BF16 = jnp.bfloat16


def _params(sem, vmem=VMEM_LIMIT):
    return pltpu.CompilerParams(dimension_semantics=sem, vmem_limit_bytes=vmem)


def _layer_spec(arr, *lead):
    tail = arr.shape[len(lead):]
    index = tuple(lead) + (0,) * len(tail)
    return pl.BlockSpec((None,) * len(lead) + tuple(tail), lambda *_: index, pipeline_mode=pl.Buffered(1))


def _mod_kernel(c_ref, w_ref, b_ref, o_ref):
    s = jax.nn.silu(c_ref[...]).astype(BF16)
    o_ref[0] = jnp.dot(s, w_ref[0].astype(BF16), preferred_element_type=F32) + b_ref[0]


def _modulation(cond, w_mod, b_mod):
    depth, d, d3 = w_mod.shape
    rows = cond.shape[0]
    nchunk = d3 // d
    return pl.pallas_call(
        _mod_kernel,
        grid=(depth, nchunk),
        in_specs=[
            pl.BlockSpec((rows, d), lambda l, j: (0, 0)),
            pl.BlockSpec((1, d, d), lambda l, j: (l, 0, j)),
            pl.BlockSpec((1, 1, d), lambda l, j: (l, 0, j)),
        ],
        out_specs=pl.BlockSpec((1, rows, d), lambda l, j: (l, 0, j)),
        out_shape=jax.ShapeDtypeStruct((depth, rows, d3), F32),
        compiler_params=_params(("arbitrary", "arbitrary")),
        name="modulation",
    )(cond, w_mod, b_mod.reshape(depth, 1, d3))


def _head_norm_rope(z, g, cos, sgn):
    h, hd, tm = z.shape
    ms = jnp.mean(z * z, axis=1, keepdims=True)
    y = z * lax.rsqrt(ms + EPS) * g
    quarter = hd // 4
    y4 = y.reshape(h * 2, 2, quarter, tm)
    swapped = jnp.concatenate([y4[:, 1:2], y4[:, 0:1]], axis=1).reshape(h, hd, tm)
    return y * cos + swapped * sgn


def _in_proj_kernel(xc_ref, xl_ref, mod_ref, ng_ref, wt_ref, wn_ref, gq_ref, gk_ref, cos_ref, sgn_ref,
                    q_ref, k_ref, v_ref, rest_ref, *, n_ctx_tiles):
    d_attn = N_Q_HEADS * HEAD_DIM
    d_kv = N_KV_HEADS * HEAD_DIM
    width = Q_PER_KV * HEAD_DIM
    tm = xl_ref.shape[1]
    n_rest = wn_ref.shape[1]
    is_ctx = pl.program_id(1) < n_ctx_tiles

    def one(bi, carry):
        x = jnp.where(is_ctx, xc_ref[bi], xl_ref[bi])
        ms = jnp.mean(x * x, axis=-1, keepdims=True)
        y = x * lax.rsqrt(ms + EPS) * ng_ref[...]
        shift = mod_ref[bi, 0, 0:1, :]
        scale = mod_ref[bi, 0, 1:2, :]
        h = (y * (1.0 + scale) + shift).astype(BF16)

        pt = lax.dot_general(wt_ref[...], h, (((1,), (1,)), ((), ())), preferred_element_type=F32)
        cos = cos_ref[...]
        sgn = sgn_ref[...]
        q = _head_norm_rope(pt[0:d_attn].reshape(N_Q_HEADS, HEAD_DIM, tm), gq_ref[...], cos, sgn)
        qs = (q * (HEAD_DIM ** -0.5 * LOG2_E)).reshape(d_attn, tm).astype(BF16)
        for j in range(N_KV_HEADS):
            q_ref[bi, j, 0] = qs[j * width:(j + 1) * width]
        k = _head_norm_rope(pt[d_attn:d_attn + d_kv].reshape(N_KV_HEADS, HEAD_DIM, tm), gk_ref[...], cos, sgn)
        kn = k.reshape(d_kv, tm).T
        for j in range(N_KV_HEADS):
            k_ref[bi, j] = kn[:, j * HEAD_DIM:(j + 1) * HEAD_DIM].astype(BF16)
            v_ref[bi, j, 0] = pt[d_attn + d_kv + j * HEAD_DIM:d_attn + d_kv + (j + 1) * HEAD_DIM].astype(BF16)

        step = 1024
        for c in range(0, n_rest, step):
            rest_ref[bi, :, c:c + step] = jnp.dot(h, wn_ref[:, c:c + step], preferred_element_type=F32)
        return carry

    lax.fori_loop(0, xl_ref.shape[0], one, 0)


def _stream_specs(nb, d, n_ctx_tiles, skip=0):
    return [
        pl.BlockSpec((nb, TILE, d), lambda i, j: (i, jnp.minimum(j + skip, n_ctx_tiles - 1), 0)),
        pl.BlockSpec((nb, TILE, d), lambda i, j: (i, jnp.maximum(j + skip - n_ctx_tiles, 0), 0)),
    ]


def _in_proj(xc, xl, prm, l, cos_t, sgn_t):
    b, n_ctx, d = xc.shape
    n_ctx_tiles = n_ctx // TILE
    t = n_ctx + xl.shape[1]
    nt = t // TILE
    d_attn = N_Q_HEADS * HEAD_DIM
    n_rest = prm["wn"].shape[2]
    nb = math.gcd(b, IN_PROJ_BATCH)
    return pl.pallas_call(
        functools.partial(_in_proj_kernel, n_ctx_tiles=n_ctx_tiles),
        grid=(b // nb, nt),
        in_specs=_stream_specs(nb, d, n_ctx_tiles) + [
            pl.BlockSpec((None, nb, 1, 3, d), lambda i, j: (l, i, jnp.where(j < n_ctx_tiles, 0, 1), 0, 0)),
            _layer_spec(prm["ng"], l),
            _layer_spec(prm["wt"], l),
            _layer_spec(prm["wn"], l),
            _layer_spec(prm["gq"], l),
            _layer_spec(prm["gk"], l),
            pl.BlockSpec((HEAD_DIM, TILE), lambda i, j: (0, j)),
            pl.BlockSpec((HEAD_DIM, TILE), lambda i, j: (0, j)),
        ],
        out_specs=[
            pl.BlockSpec((nb, N_KV_HEADS, 1, d_attn // N_KV_HEADS, TILE), lambda i, j: (i, 0, j, 0, 0)),
            pl.BlockSpec((nb, N_KV_HEADS, TILE, HEAD_DIM), lambda i, j: (i, 0, j, 0)),
            pl.BlockSpec((nb, N_KV_HEADS, 1, HEAD_DIM, TILE), lambda i, j: (i, 0, j, 0, 0)),
            pl.BlockSpec((nb, TILE, n_rest), lambda i, j: (i, j, 0)),
        ],
        out_shape=[
            jax.ShapeDtypeStruct((b, N_KV_HEADS, nt, d_attn // N_KV_HEADS, TILE), BF16),
            jax.ShapeDtypeStruct((b, N_KV_HEADS, t, HEAD_DIM), BF16),
            jax.ShapeDtypeStruct((b, N_KV_HEADS, nt, HEAD_DIM, TILE), BF16),
            jax.ShapeDtypeStruct((b, t, n_rest), F32),
        ],
        compiler_params=_params(("parallel", "parallel")),
        name="in_proj",
    )(xc, xl, prm["modsel"], prm["ng"], prm["wt"], prm["wn"], prm["gq"], prm["gk"], cos_t, sgn_t)


def _attn_kernel(q_ref, k_ref, v_ref, ga_ref, o_ref, s_ref, mx_ref, m_ref, acc_ref,
                 *, n_ctx_tiles, n_tiles):
    ones_rows = jnp.ones((SUM_ROWS, TILE), BF16)

    def scores(qt, c, slot):
        kc = k_ref[0, 0, c * TILE:(c + 1) * TILE, :]
        for g in range(Q_PER_KV):
            qg = q_ref[0, 0, qt, g * HEAD_DIM:(g + 1) * HEAD_DIM, :]
            s = jnp.dot(kc, qg, preferred_element_type=F32)
            s_ref[slot, g] = s
            mx_ref[slot, g] = jnp.max(s, axis=0, keepdims=True)

    def update(c, slot, acc_set, first):
        vc = jnp.concatenate([v_ref[0, 0, c], ones_rows], axis=0)
        for g in range(Q_PER_KV):
            if first:
                m_new = mx_ref[slot, g]
                p = jnp.exp2(s_ref[slot, g] - m_new)
                acc_ref[acc_set, g] = jnp.dot(vc, p.astype(BF16), preferred_element_type=F32)
            else:
                m_prev = m_ref[acc_set, g]
                m_new = jnp.maximum(m_prev, mx_ref[slot, g])
                alpha = jnp.exp2(m_prev - m_new)
                p = jnp.exp2(s_ref[slot, g] - m_new)
                acc_ref[acc_set, g] = (alpha * acc_ref[acc_set, g]
                                       + jnp.dot(vc, p.astype(BF16), preferred_element_type=F32))
            m_ref[acc_set, g] = m_new

    def finish(qt, acc_set):
        o = jnp.concatenate([acc_ref[acc_set, g, 0:HEAD_DIM] / acc_ref[acc_set, g, HEAD_DIM:HEAD_DIM + 1]
                             for g in range(Q_PER_KV)], axis=0)
        if isinstance(qt, int):
            rows = slice(qt * TILE, (qt + 1) * TILE)
        else:
            rows = pl.ds(pl.multiple_of(qt * TILE, TILE), TILE)
        ga = ga_ref[0, rows, :]
        o_ref[0, rows, :] = (o.T * (ga * _sigmoid(ga))).astype(BF16)

    def tile(qt, parity, chunks, next_qt, next_chunk):
        for idx, c in enumerate(chunks):
            slot = (parity + idx) % 2
            if idx + 1 < len(chunks):
                scores(qt, chunks[idx + 1], 1 - slot)
            else:
                scores(next_qt, next_chunk, 1 - slot)
            update(c, slot, parity, idx == 0)
        finish(qt, parity)

    ctx_chunks = list(range(n_ctx_tiles))
    all_chunks = list(range(n_tiles))
    parity = 0
    scores(0, 0, parity)
    for qt in range(n_ctx_tiles):
        tile(qt, parity, ctx_chunks, qt + 1, 0)
        parity = (parity + len(ctx_chunks)) % 2
    n_latent = n_tiles - n_ctx_tiles
    pairs = n_latent // 2
    odd = len(all_chunks) % 2
    first_parity = parity

    def pair(i, carry):
        qa = n_ctx_tiles + 2 * i
        tile(qa, first_parity, all_chunks, qa + 1, 0)
        tile(qa + 1, (first_parity + odd) % 2, all_chunks, jnp.minimum(qa + 2, n_tiles - 1), 0)
        return carry

    lax.fori_loop(0, pairs, pair, 0)


def _attention(q5, k4, vt, rest, ga_col, n_ctx_tiles):
    b, _, nt, width, _ = q5.shape
    t = nt * TILE
    d_attn = N_KV_HEADS * width
    assert (nt - n_ctx_tiles) % 2 == 0
    kern = functools.partial(_attn_kernel, n_ctx_tiles=n_ctx_tiles, n_tiles=nt)
    return pl.pallas_call(
        kern,
        grid=(b, N_KV_HEADS),
        in_specs=[
            pl.BlockSpec((1, 1, nt, width, TILE), lambda i, j: (i, j, 0, 0, 0)),
            pl.BlockSpec((1, 1, t, HEAD_DIM), lambda i, j: (i, j, 0, 0)),
            pl.BlockSpec((1, 1, nt, HEAD_DIM, TILE), lambda i, j: (i, j, 0, 0, 0)),
            pl.BlockSpec((1, t, width), lambda i, j: (i, 0, ga_col // width + j)),
        ],
        out_specs=pl.BlockSpec((1, t, width), lambda i, j: (i, 0, j)),
        out_shape=jax.ShapeDtypeStruct((b, t, d_attn), BF16),
        scratch_shapes=[
            pltpu.VMEM((2, Q_PER_KV, TILE, TILE), F32),
            pltpu.VMEM((2, Q_PER_KV, 1, TILE), F32),
            pltpu.VMEM((2, Q_PER_KV, 1, TILE), F32),
            pltpu.VMEM((2, Q_PER_KV, HEAD_DIM + SUM_ROWS, TILE), F32),
        ],
        compiler_params=_params(("parallel", "parallel")),
        name="attention",
    )(q5, k4, vt, rest)


def _softplus(x):
    return jnp.maximum(x, 0.0) + jnp.log1p(jnp.exp(-jnp.abs(x)))


def _sigmoid(x):
    return 0.5 + 0.5 * jnp.tanh(0.5 * x)


def _lru_tile_index(i, n_ctx_tiles, n_tiles, reverse):
    if not reverse:
        return i
    return jnp.where(i < n_ctx_tiles, n_ctx_tiles - 1 - i, n_tiles - 1 - (i - n_ctx_tiles))


def _lru_kernel(*refs, n_ctx_tiles, n_tiles, reverse):
    carry_ref = refs[11 if reverse else 9]

    @pl.when(pl.program_id(1) == 0)
    def _():
        carry_ref[...] = jnp.zeros(carry_ref.shape, F32)

    def one(bi, carry):
        _lru_tile(bi, refs, n_ctx_tiles, n_tiles, reverse)
        return carry

    lax.fori_loop(0, refs[0].shape[0], one, 0)


def _lru_tile(bi, refs, n_ctx_tiles, n_tiles, reverse):
    if reverse:
        (u_ref, up_ref, un_ref, cw_ref, cb_ref, wg_ref, bg_ref, lam_ref, hf_ref, gb_ref,
         o_ref, carry_ref, ext_ref, uc_ref, a_ref, b_ref, h_ref) = refs
    else:
        (u_ref, up_ref, un_ref, cw_ref, cb_ref, wg_ref, bg_ref, lam_ref,
         o_ref, carry_ref, ext_ref, uc_ref, a_ref, b_ref, h_ref) = refs
    tb = _lru_tile_index(pl.program_id(1), n_ctx_tiles, n_tiles, reverse)
    d = u_ref.shape[2]
    n_lane_tiles = d // LANES
    per_group = LRU_GROUP // LANES
    n_sub = EXT_ROWS // LRU_PITCH
    has_prev = jnp.logical_and(tb != 0, tb != n_ctx_tiles)
    has_next = jnp.logical_and(tb != n_ctx_tiles - 1, tb != n_tiles - 1)
    zeros = jnp.zeros((HALO, LANES), F32)
    for lt in range(n_lane_tiles):
        sl = slice(lt * LANES, (lt + 1) * LANES)
        ext_ref[lt, 0:HALO] = zeros
        ext_ref[lt, HALO:2 * HALO] = jnp.where(has_prev, up_ref[bi, :, sl], 0.0)
        ext_ref[lt, 2 * HALO:2 * HALO + TILE] = u_ref[bi, :, sl]
        ext_ref[lt, 2 * HALO + TILE:3 * HALO + TILE] = jnp.where(has_next, un_ref[bi, :, sl], 0.0)
        ext_ref[lt, 3 * HALO + TILE:4 * HALO + TILE] = zeros

    p = lax.broadcasted_iota(jnp.int32, (n_sub * HALO, LRU_GROUP), 0)
    valid_head = LRU_PITCH * (p % n_sub) + p // n_sub >= HALO
    valid_tail = LRU_PITCH * (p % n_sub) + p // n_sub + (LRU_PITCH - HALO) < HALO + TILE

    for cg in range(n_lane_tiles // per_group):
        lo, hi = cg * LRU_GROUP, (cg + 1) * LRU_GROUP
        for hh in range(per_group):
            lt = cg * per_group + hh
            sl = slice(lt * LANES, (lt + 1) * LANES)
            taps = [cw_ref[j:j + 1, sl] for j in range(CONV_W)]
            bias = cb_ref[:, sl]
            for r in range(LRU_PITCH):
                acc = None
                for j in range(CONV_W):
                    x = ext_ref[lt, pl.ds(HALO + r + j - CONV_PAD_LEFT, n_sub, stride=LRU_PITCH), :]
                    acc = x * taps[j] if acc is None else acc + x * taps[j]
                uc_ref[r * n_sub:(r + 1) * n_sub, hh * LANES:(hh + 1) * LANES] = acc + bias
        uc = uc_ref[...]
        pre = jnp.dot(uc.astype(BF16), wg_ref[cg], preferred_element_type=F32)
        tr = jnp.tanh(pre[:, :LRU_GROUP] + 0.5 * bg_ref[0:1, lo:hi])
        ti = jnp.tanh(pre[:, LRU_GROUP:] + 0.5 * bg_ref[1:2, lo:hi])
        half_c = (-0.5 * LRU_C) * _softplus(-lam_ref[:, lo:hi])
        log_a = half_c + half_c * tr
        a = jnp.exp(log_a)
        th = jnp.tanh(log_a)
        zz = (-0.5 * th) / (1.0 - th)
        root = zz * lax.rsqrt(jnp.maximum(zz, F32_MIN_NORMAL))
        bb = (root * uc) * (1.0 + ti)
        head, tail = n_sub * HALO, n_sub * (LRU_PITCH - HALO)
        a_ref[0:head] = jnp.where(valid_head, a[0:head], 1.0)
        b_ref[0:head] = jnp.where(valid_head, bb[0:head], 0.0)
        a_ref[head:tail] = a[head:tail]
        b_ref[head:tail] = bb[head:tail]
        a_ref[tail:] = jnp.where(valid_tail, a[tail:], 1.0)
        b_ref[tail:] = jnp.where(valid_tail, bb[tail:], 0.0)

        for hh in range(per_group):
            lt = cg * per_group + hh
            sl = slice(lt * LANES, (lt + 1) * LANES)
            cols = slice(hh * LANES, (hh + 1) * LANES)
            order = range(LRU_PITCH - 1, -1, -1) if reverse else range(LRU_PITCH)
            h_run = a_run = None
            for r in order:
                rows = slice(r * n_sub, (r + 1) * n_sub)
                a_r = a_ref[rows, cols]
                b_r = b_ref[rows, cols]
                h_run = b_r if h_run is None else a_r * h_run + b_r
                a_run = a_r if a_run is None else a_run * a_r
                b_ref[rows, cols] = h_run
                a_ref[rows, cols] = a_run
            state = carry_ref[bi, :, sl]
            carry_in = [None] * n_sub
            for s in (range(n_sub - 1, -1, -1) if reverse else range(n_sub)):
                carry_in[s] = state
                state = a_run[s:s + 1] * state + h_run[s:s + 1]
            carry_ref[bi, :, sl] = state
            cin = jnp.concatenate(carry_in, axis=0)
            for r in range(LRU_PITCH):
                rows = slice(r * n_sub, (r + 1) * n_sub)
                h_ref[lt, pl.ds(r, n_sub, stride=LRU_PITCH), :] = b_ref[rows, cols] + a_ref[rows, cols] * cin

    for lt in range(n_lane_tiles):
        sl = slice(lt * LANES, (lt + 1) * LANES)
        h = h_ref[lt, HALO:HALO + TILE, :]
        if reverse:
            g = gb_ref[bi, :, sl]
            o_ref[bi, :, sl] = ((hf_ref[bi, :, sl] + h) * (g * _sigmoid(g))).astype(BF16)
        else:
            o_ref[bi, :, sl] = h


def _lru(rest, u_col, gb_col, prm, l, dr, hf, n_ctx_tiles, reverse):
    b, t, _ = rest.shape
    d = prm["conv_w"].shape[2]
    nt = t // TILE
    per_tile = TILE // HALO
    n_halo_blocks = t // HALO
    tile = functools.partial(_lru_tile_index, n_ctx_tiles=n_ctx_tiles, n_tiles=nt, reverse=reverse)
    ucol = u_col // d
    nb = math.gcd(b, LRU_BATCH)
    in_specs = [
        pl.BlockSpec((nb, TILE, d), lambda i, j: (i, tile(j), ucol)),
        pl.BlockSpec((nb, HALO, d), lambda i, j: (i, jnp.maximum(tile(j) * per_tile - 1, 0), ucol)),
        pl.BlockSpec((nb, HALO, d),
                     lambda i, j: (i, jnp.minimum((tile(j) + 1) * per_tile, n_halo_blocks - 1), ucol)),
        _layer_spec(prm["conv_w"], l),
        _layer_spec(prm["conv_b"], l),
        _layer_spec(prm["wg"], l, dr),
        _layer_spec(prm["bg"], l, dr),
        _layer_spec(prm["lam"], l, dr),
    ]
    args = [rest, rest, rest, prm["conv_w"], prm["conv_b"], prm["wg"], prm["bg"], prm["lam"]]
    if reverse:
        in_specs += [
            pl.BlockSpec((nb, TILE, d), lambda i, j: (i, tile(j), 0)),
            pl.BlockSpec((nb, TILE, d), lambda i, j: (i, tile(j), gb_col // d)),
        ]
        args += [hf, rest]
    kern = functools.partial(_lru_kernel, n_ctx_tiles=n_ctx_tiles, n_tiles=nt, reverse=reverse)
    return pl.pallas_call(
        kern,
        grid=(b // nb, nt),
        in_specs=in_specs,
        out_specs=pl.BlockSpec((nb, TILE, d), lambda i, j: (i, tile(j), 0)),
        out_shape=jax.ShapeDtypeStruct((b, t, d), BF16 if reverse else F32),
        scratch_shapes=[
            pltpu.VMEM((nb, 1, d), F32),
            pltpu.VMEM((d // LANES, EXT_ROWS + 2 * HALO, LANES), F32),
            pltpu.VMEM((EXT_ROWS, LRU_GROUP), F32),
            pltpu.VMEM((EXT_ROWS, LRU_GROUP), F32),
            pltpu.VMEM((EXT_ROWS, LRU_GROUP), F32),
            pltpu.VMEM((d // LANES, EXT_ROWS, LANES), F32),
        ],
        compiler_params=_params(("parallel", "arbitrary")),
        name="lru_rev" if reverse else "lru_fwd",
    )(*args)


def _merge_kernel(*refs, n_ctx_tiles, skip):
    if skip:
        oa_ref, ob_ref, gm_ref, xc_ref, xl_ref, mod_ref, wa_ref, wb_ref, wo_ref, ol_ref = refs
        oc_ref = None
    else:
        oa_ref, ob_ref, gm_ref, xc_ref, xl_ref, mod_ref, wa_ref, wb_ref, wo_ref, oc_ref, ol_ref = refs
    d = xl_ref.shape[2]
    is_ctx = pl.program_id(1) + skip < n_ctx_tiles

    def one(bi, carry):
        ya = jnp.dot(oa_ref[bi], wa_ref[...], preferred_element_type=F32)
        yb = jnp.dot(ob_ref[bi], wb_ref[...], preferred_element_type=F32)
        mix = _sigmoid(gm_ref[bi, :, 0:d]) * ya + _sigmoid(gm_ref[bi, :, d:2 * d]) * yb
        out = mod_ref[bi, 0, 2:3, :] * jnp.dot(mix.astype(BF16), wo_ref[...], preferred_element_type=F32)
        if oc_ref is None:
            ol_ref[bi] = xl_ref[bi] + out
        else:
            ol_ref[bi] = jnp.where(is_ctx, xc_ref[bi], xl_ref[bi]) + out

            @pl.when(is_ctx)
            def _():
                oc_ref[bi] = ol_ref[bi]
        return carry

    lax.fori_loop(0, xl_ref.shape[0], one, 0)


def _merge(oa, ob, rest, xc, xl, prm, l, latent_only):
    b, n_ctx, d = xc.shape
    n_ctx_tiles = n_ctx // TILE
    nt = n_ctx_tiles + xl.shape[1] // TILE
    skip = n_ctx_tiles if latent_only else 0
    nb = math.gcd(b, MERGE_BATCH)
    ctx_spec, lat_spec = _stream_specs(nb, d, n_ctx_tiles, skip)
    out_specs = _stream_specs(nb, d, n_ctx_tiles, skip)
    out_shape = [jax.ShapeDtypeStruct(xc.shape, F32), jax.ShapeDtypeStruct(xl.shape, F32)]
    if latent_only:
        out_specs, out_shape = out_specs[1:], out_shape[1:]
    outs = pl.pallas_call(
        functools.partial(_merge_kernel, n_ctx_tiles=n_ctx_tiles, skip=skip),
        grid=(b // nb, nt - skip),
        in_specs=[
            pl.BlockSpec((nb, TILE, d), lambda i, j: (i, j + skip, 0)),
            pl.BlockSpec((nb, TILE, d), lambda i, j: (i, j + skip, 0)),
            pl.BlockSpec((nb, TILE, 2 * d), lambda i, j: (i, j + skip, 0)),
            ctx_spec,
            lat_spec,
            pl.BlockSpec((None, nb, 1, 3, d),
                         lambda i, j: (l, i, jnp.where(j + skip < n_ctx_tiles, 0, 1), 0, 0)),
            _layer_spec(prm["wa"], l),
            _layer_spec(prm["wb"], l),
            _layer_spec(prm["wo"], l),
        ],
        out_specs=out_specs,
        out_shape=out_shape,
        compiler_params=_params(("parallel", "arbitrary")),
        name="merge",
    )(oa, ob, rest, xc, xl, prm["modsel"], prm["wa"], prm["wb"], prm["wo"])
    return (None, outs[0]) if latent_only else tuple(outs)


def _rope_tables(n, n_ctx):
    quarter = HEAD_DIM // 4
    freqs = ROPE_THETA ** (-jnp.arange(quarter, dtype=F32) / quarter)
    pos = jnp.arange(n)
    ang_r = (pos // GRID_W).astype(F32)[None, :] * freqs[:, None]
    ang_c = (pos % GRID_W).astype(F32)[None, :] * freqs[:, None]
    cos = jnp.concatenate([jnp.cos(ang_r)] * 2 + [jnp.cos(ang_c)] * 2, axis=0)
    sgn = jnp.concatenate([-jnp.sin(ang_r), jnp.sin(ang_r), -jnp.sin(ang_c), jnp.sin(ang_c)], axis=0)
    cos = jnp.concatenate([jnp.ones((HEAD_DIM, n_ctx), F32), cos], axis=1)
    sgn = jnp.concatenate([jnp.zeros((HEAD_DIM, n_ctx), F32), sgn], axis=1)
    return cos, sgn


def _gate_weights(w):
    lead = w.shape[:-4]
    blk = w.shape[-1]
    per = LRU_GROUP // blk
    n_groups = LRU_BLOCKS // per
    wg = w.reshape(lead + (2, n_groups, per, blk, blk))
    eye = jnp.eye(per, dtype=w.dtype)
    dense = jnp.einsum('...gnpij,pq->...gnpiqj', wg, eye).reshape(lead + (2, n_groups, per * blk, per * blk))
    return (0.5 * jnp.concatenate([dense[..., 0, :, :, :], dense[..., 1, :, :, :]], axis=-1)).astype(BF16)


def kernel(x, c, ctx, c_ctx, norm_g, w_mod, b_mod, w_in, q_norm_g, k_norm_g, conv_w, conv_b,
           lru_gate_w, lru_gate_b, lru_lambda, w_a_out, w_b_out, w_out):
    bsz, n, d = x.shape
    n_ctx = ctx.shape[1]
    depth = w_in.shape[0]
    assert n % TILE == 0 and n_ctx % TILE == 0 and n % GRID_W == 0
    d_attn = N_Q_HEADS * HEAD_DIM
    d_kv = N_KV_HEADS * HEAD_DIM
    d_lru = conv_w.shape[-1]
    assert d_lru == d and d_attn == d and w_in.shape[2] == 2 * d_attn + 2 * d_kv + 2 * d_lru + 2 * d
    n_ctx_tiles = n_ctx // TILE

    rows = -(-(bsz + 1) // 8) * 8
    cond = jnp.zeros((rows, d), F32).at[:bsz].set(c).at[bsz].set(c_ctx)
    mod = _modulation(cond, w_mod, b_mod)
    cos_t, sgn_t = _rope_tables(n, n_ctx)

    o_q = 0
    o_ga = d_attn + 2 * d_kv
    o_u = o_ga + d_attn
    o_gb = o_u + d_lru
    o_gm = o_gb + d_lru
    ga_col, u_col, gb_col = 2 * d, 2 * d + d_attn, 2 * d + d_attn + d_lru

    m3 = mod.reshape(depth, rows, 3, d)
    modsel = jnp.stack([jnp.broadcast_to(m3[:, bsz:bsz + 1], (depth, bsz, 3, d)), m3[:, :bsz]],
                       axis=2)
    wt = jnp.swapaxes(w_in[:, :, o_q:o_ga], 1, 2).astype(BF16)
    wn = jnp.concatenate([w_in[:, :, o_gm:], w_in[:, :, o_ga:o_u], w_in[:, :, o_u:o_gb],
                          w_in[:, :, o_gb:o_gm]], axis=2).astype(BF16)
    prm = dict(
        ng=norm_g[:, None, :], wt=wt, wn=wn, gq=q_norm_g[:, :, None], gk=k_norm_g[:, :, None],
        conv_w=conv_w, conv_b=conv_b[:, None, :], wg=_gate_weights(lru_gate_w), bg=lru_gate_b,
        lam=lru_lambda[:, :, None, :], wa=w_a_out.astype(BF16), wb=w_b_out.astype(BF16),
        wo=w_out.astype(BF16), modsel=modsel)

    xc, xl = ctx, x
    for l in range(depth):
        qt, k4, vt, rest = _in_proj(xc, xl, prm, l, cos_t, sgn_t)
        oa = _attention(qt, k4, vt, rest, ga_col, n_ctx_tiles)
        hf = None
        for dr, rev in enumerate((False, True)):
            hf = _lru(rest, u_col, gb_col, prm, l, dr, hf, n_ctx_tiles, rev)
        xc, xl = _merge(oa, hf, rest, xc, xl, prm, l, latent_only=(l == depth - 1))
    return xl
```

```python
import functools
import math

import jax
import jax.numpy as jnp
from jax import lax
from jax.experimental import pallas as pl
from jax.experimental.pallas import tpu as pltpu

GRID_W = 64
HEAD_DIM = 64
N_Q_HEADS = 16
N_KV_HEADS = 4
Q_PER_KV = N_Q_HEADS // N_KV_HEADS
LRU_BLOCKS = 16
CONV_W = 4
CONV_PAD_LEFT = 2
LRU_C = 8.0
ROPE_THETA = 10000.0
EPS = 1e-6

TILE = 256
HALO = 8
LANES = 128
LRU_GROUP = 256
EXT_ROWS = TILE + 2 * HALO
LRU_PITCH = EXT_ROWS // 8
assert LRU_PITCH * 8 == EXT_ROWS and LRU_PITCH % 4 != 0
IN_PROJ_BATCH = 2
MERGE_BATCH = 2
LRU_BATCH = 4
SUM_ROWS = 16
LOG2_E = 1.4426950408889634
F32_MIN_NORMAL = 1.1754943508222875e-38
VMEM_LIMIT = 56 * 1024 * 1024

F32 = jnp.float32
BF16 = jnp.bfloat16


def _params(sem, vmem=VMEM_LIMIT):
    return pltpu.CompilerParams(dimension_semantics=sem, vmem_limit_bytes=vmem)


def _layer_spec(arr, *lead):
    tail = arr.shape[len(lead):]
    index = tuple(lead) + (0,) * len(tail)
    return pl.BlockSpec((None,) * len(lead) + tuple(tail), lambda *_: index, pipeline_mode=pl.Buffered(1))


def _mod_kernel(c_ref, w_ref, b_ref, o_ref):
    s = jax.nn.silu(c_ref[...]).astype(BF16)
    o_ref[0] = jnp.dot(s, w_ref[0].astype(BF16), preferred_element_type=F32) + b_ref[0]


def _modulation(cond, w_mod, b_mod):
    depth, d, d3 = w_mod.shape
    rows = cond.shape[0]
    nchunk = d3 // d
    return pl.pallas_call(
        _mod_kernel,
        grid=(depth, nchunk),
        in_specs=[
            pl.BlockSpec((rows, d), lambda l, j: (0, 0)),
            pl.BlockSpec((1, d, d), lambda l, j: (l, 0, j)),
            pl.BlockSpec((1, 1, d), lambda l, j: (l, 0, j)),
        ],
        out_specs=pl.BlockSpec((1, rows, d), lambda l, j: (l, 0, j)),
        out_shape=jax.ShapeDtypeStruct((depth, rows, d3), F32),
        compiler_params=_params(("arbitrary", "arbitrary")),
        name="modulation",
    )(cond, w_mod, b_mod.reshape(depth, 1, d3))


def _head_norm_rope(z, tab_a, tab_b):
    h, hd, tm = z.shape
    ms = jnp.mean(z * z, axis=1, keepdims=True)
    y = z * lax.rsqrt(ms + EPS)
    quarter = hd // 4
    y4 = y.reshape(h * 2, 2, quarter, tm)
    swapped = jnp.concatenate([y4[:, 1:2], y4[:, 0:1]], axis=1).reshape(h, hd, tm)
    return y * tab_a + swapped * tab_b


def _in_proj_kernel(xc_ref, xl_ref, mod_ref, ng_ref, wt_ref, wn_ref, tab_ref,
                    q_ref, k_ref, v_ref, rest_ref, *, n_ctx_tiles):
    d_attn = N_Q_HEADS * HEAD_DIM
    d_kv = N_KV_HEADS * HEAD_DIM
    width = Q_PER_KV * HEAD_DIM
    tm = xl_ref.shape[1]
    n_rest = wn_ref.shape[1]
    is_ctx = pl.program_id(1) < n_ctx_tiles

    def one(bi, carry):
        x = jnp.where(is_ctx, xc_ref[bi], xl_ref[bi])
        ms = jnp.mean(x * x, axis=-1, keepdims=True)
        y = x * lax.rsqrt(ms + EPS) * ng_ref[...]
        shift = mod_ref[bi, 0, 0:1, :]
        scale = mod_ref[bi, 0, 1:2, :]
        h = (y * (1.0 + scale) + shift).astype(BF16)

        pt = lax.dot_general(wt_ref[...], h, (((1,), (1,)), ((), ())), preferred_element_type=F32)
        q = _head_norm_rope(pt[0:d_attn].reshape(N_Q_HEADS, HEAD_DIM, tm), tab_ref[0], tab_ref[1])
        qs = q.reshape(d_attn, tm).astype(BF16)
        for j in range(N_KV_HEADS):
            q_ref[bi, j, 0] = qs[j * width:(j + 1) * width]
        k = _head_norm_rope(pt[d_attn:d_attn + d_kv].reshape(N_KV_HEADS, HEAD_DIM, tm), tab_ref[2], tab_ref[3])
        kn = k.reshape(d_kv, tm).T
        for j in range(N_KV_HEADS):
            k_ref[bi, j] = kn[:, j * HEAD_DIM:(j + 1) * HEAD_DIM].astype(BF16)
            v_ref[bi, j, 0] = pt[d_attn + d_kv + j * HEAD_DIM:d_attn + d_kv + (j + 1) * HEAD_DIM].astype(BF16)

        step = 1024
        for c in range(0, n_rest, step):
            rest_ref[bi, :, c:c + step] = jnp.dot(h, wn_ref[:, c:c + step], preferred_element_type=F32)
        return carry

    for bi in range(xl_ref.shape[0]):
        one(bi, 0)


def _stream_specs(nb, d, n_ctx_tiles, skip=0):
    return [
        pl.BlockSpec((nb, TILE, d), lambda i, j: (i, jnp.minimum(j + skip, n_ctx_tiles - 1), 0)),
        pl.BlockSpec((nb, TILE, d), lambda i, j: (i, jnp.maximum(j + skip - n_ctx_tiles, 0), 0)),
    ]


def _in_proj(xc, xl, prm, l):
    b, n_ctx, d = xc.shape
    n_ctx_tiles = n_ctx // TILE
    t = n_ctx + xl.shape[1]
    nt = t // TILE
    d_attn = N_Q_HEADS * HEAD_DIM
    n_rest = prm["wn"].shape[2]
    nb = math.gcd(b, IN_PROJ_BATCH)
    return pl.pallas_call(
        functools.partial(_in_proj_kernel, n_ctx_tiles=n_ctx_tiles),
        grid=(b // nb, nt),
        in_specs=_stream_specs(nb, d, n_ctx_tiles) + [
            pl.BlockSpec((None, nb, 1, 3, d), lambda i, j: (l, i, jnp.where(j < n_ctx_tiles, 0, 1), 0, 0)),
            _layer_spec(prm["ng"], l),
            _layer_spec(prm["wt"], l),
            _layer_spec(prm["wn"], l),
            pl.BlockSpec((None, 4, HEAD_DIM, TILE), lambda i, j: (l, 0, 0, j)),
        ],
        out_specs=[
            pl.BlockSpec((nb, N_KV_HEADS, 1, d_attn // N_KV_HEADS, TILE), lambda i, j: (i, 0, j, 0, 0)),
            pl.BlockSpec((nb, N_KV_HEADS, TILE, HEAD_DIM), lambda i, j: (i, 0, j, 0)),
            pl.BlockSpec((nb, N_KV_HEADS, 1, HEAD_DIM, TILE), lambda i, j: (i, 0, j, 0, 0)),
            pl.BlockSpec((nb, TILE, n_rest), lambda i, j: (i, j, 0)),
        ],
        out_shape=[
            jax.ShapeDtypeStruct((b, N_KV_HEADS, nt, d_attn // N_KV_HEADS, TILE), BF16),
            jax.ShapeDtypeStruct((b, N_KV_HEADS, t, HEAD_DIM), BF16),
            jax.ShapeDtypeStruct((b, N_KV_HEADS, nt, HEAD_DIM, TILE), BF16),
            jax.ShapeDtypeStruct((b, t, n_rest), F32),
        ],
        compiler_params=_params(("parallel", "parallel")),
        name="in_proj",
    )(xc, xl, prm["modsel"], prm["ng"], prm["wt"], prm["wn"], prm["rope"])


def _attn_kernel(q_ref, k_ref, v_ref, ga_ref, o_ref, s_ref, mx_ref, m_ref, acc_ref,
                 *, n_ctx_tiles, n_tiles):
    ones_rows = jnp.ones((SUM_ROWS, TILE), BF16)

    def scores(qt, c, slot):
        kc = k_ref[0, 0, c * TILE:(c + 1) * TILE, :]
        for g in range(Q_PER_KV):
            qg = q_ref[0, 0, qt, g * HEAD_DIM:(g + 1) * HEAD_DIM, :]
            s = jnp.dot(kc, qg, preferred_element_type=F32)
            s_ref[slot, g] = s
            mx_ref[slot, g] = jnp.max(s, axis=0, keepdims=True)

    def update(c, slot, acc_set, first):
        vc = jnp.concatenate([v_ref[0, 0, c], ones_rows], axis=0)
        for g in range(Q_PER_KV):
            if first:
                m_new = mx_ref[slot, g]
                p = jnp.exp2(s_ref[slot, g] - m_new)
                acc_ref[acc_set, g] = jnp.dot(vc, p.astype(BF16), preferred_element_type=F32)
            else:
                m_prev = m_ref[acc_set, g]
                m_new = jnp.maximum(m_prev, mx_ref[slot, g])
                alpha = jnp.exp2(m_prev - m_new)
                p = jnp.exp2(s_ref[slot, g] - m_new)
                acc_ref[acc_set, g] = (alpha * acc_ref[acc_set, g]
                                       + jnp.dot(vc, p.astype(BF16), preferred_element_type=F32))
            m_ref[acc_set, g] = m_new

    def finish(qt, acc_set):
        o = jnp.concatenate([acc_ref[acc_set, g, 0:HEAD_DIM] / acc_ref[acc_set, g, HEAD_DIM:HEAD_DIM + 1]
                             for g in range(Q_PER_KV)], axis=0)
        if isinstance(qt, int):
            rows = slice(qt * TILE, (qt + 1) * TILE)
        else:
            rows = pl.ds(pl.multiple_of(qt * TILE, TILE), TILE)
        ga = ga_ref[0, rows, :]
        o_ref[0, rows, :] = (o.T * (ga * _sigmoid(ga))).astype(BF16)

    def tile(qt, parity, chunks, next_qt, next_chunk):
        for idx, c in enumerate(chunks):
            slot = (parity + idx) % 2
            if idx + 1 < len(chunks):
                scores(qt, chunks[idx + 1], 1 - slot)
            else:
                scores(next_qt, next_chunk, 1 - slot)
            update(c, slot, parity, idx == 0)
        finish(qt, parity)

    ctx_chunks = list(range(n_ctx_tiles))
    all_chunks = list(range(n_tiles))
    parity = 0
    scores(0, 0, parity)
    for qt in range(n_ctx_tiles):
        tile(qt, parity, ctx_chunks, qt + 1, 0)
        parity = (parity + len(ctx_chunks)) % 2
    n_latent = n_tiles - n_ctx_tiles
    pairs = n_latent // 2
    odd = len(all_chunks) % 2
    first_parity = parity

    def pair(i, carry):
        qa = n_ctx_tiles + 2 * i
        tile(qa, first_parity, all_chunks, qa + 1, 0)
        tile(qa + 1, (first_parity + odd) % 2, all_chunks, jnp.minimum(qa + 2, n_tiles - 1), 0)
        return carry

    lax.fori_loop(0, pairs, pair, 0)


def _attention(q5, k4, vt, rest, ga_col, n_ctx_tiles):
    b, _, nt, width, _ = q5.shape
    t = nt * TILE
    d_attn = N_KV_HEADS * width
    assert (nt - n_ctx_tiles) % 2 == 0
    kern = functools.partial(_attn_kernel, n_ctx_tiles=n_ctx_tiles, n_tiles=nt)
    return pl.pallas_call(
        kern,
        grid=(b, N_KV_HEADS),
        in_specs=[
            pl.BlockSpec((1, 1, nt, width, TILE), lambda i, j: (i, j, 0, 0, 0)),
            pl.BlockSpec((1, 1, t, HEAD_DIM), lambda i, j: (i, j, 0, 0)),
            pl.BlockSpec((1, 1, nt, HEAD_DIM, TILE), lambda i, j: (i, j, 0, 0, 0)),
            pl.BlockSpec((1, t, width), lambda i, j: (i, 0, ga_col // width + j)),
        ],
        out_specs=pl.BlockSpec((1, t, width), lambda i, j: (i, 0, j)),
        out_shape=jax.ShapeDtypeStruct((b, t, d_attn), BF16),
        scratch_shapes=[
            pltpu.VMEM((2, Q_PER_KV, TILE, TILE), F32),
            pltpu.VMEM((2, Q_PER_KV, 1, TILE), F32),
            pltpu.VMEM((2, Q_PER_KV, 1, TILE), F32),
            pltpu.VMEM((2, Q_PER_KV, HEAD_DIM + SUM_ROWS, TILE), F32),
        ],
        compiler_params=_params(("parallel", "parallel")),
        name="attention",
    )(q5, k4, vt, rest)


def _softplus(x):
    return jnp.maximum(x, 0.0) + jnp.log1p(jnp.exp(-jnp.abs(x)))


def _sigmoid(x):
    return 0.5 + 0.5 * jnp.tanh(0.5 * x)


def _lru_tile_index(i, n_ctx_tiles, n_tiles, reverse):
    if not reverse:
        return i
    return jnp.where(i < n_ctx_tiles, n_ctx_tiles - 1 - i, n_tiles - 1 - (i - n_ctx_tiles))


def _lru_kernel(*refs, n_ctx_tiles, n_tiles, reverse):
    carry_ref = refs[7 if reverse else 10]

    @pl.when(pl.program_id(1) == 0)
    def _():
        carry_ref[...] = jnp.zeros(carry_ref.shape, F32)

    def one(bi, carry):
        _lru_tile(bi, refs, n_ctx_tiles, n_tiles, reverse)
        return carry

    lax.fori_loop(0, refs[0].shape[0], one, 0)


def _lru_tile(bi, refs, n_ctx_tiles, n_tiles, reverse):
    if reverse:
        (ucm_ref, wg_ref, bg_ref, lam_ref, hf_ref, gb_ref,
         o_ref, carry_ref, a_ref, b_ref, h_ref) = refs
        d = ucm_ref.shape[3]
    else:
        (u_ref, up_ref, un_ref, cw_ref, cb_ref, wg_ref, bg_ref, lam_ref,
         o_ref, ucm_ref, carry_ref, ext_ref, uc_ref, a_ref, b_ref, h_ref) = refs
        d = u_ref.shape[2]
    tb = _lru_tile_index(pl.program_id(1), n_ctx_tiles, n_tiles, reverse)
    n_lane_tiles = d // LANES
    per_group = LRU_GROUP // LANES
    n_sub = EXT_ROWS // LRU_PITCH
    if not reverse:
        has_prev = jnp.logical_and(tb != 0, tb != n_ctx_tiles)
        has_next = jnp.logical_and(tb != n_ctx_tiles - 1, tb != n_tiles - 1)
        zeros = jnp.zeros((HALO, LANES), F32)
        for lt in range(n_lane_tiles):
            sl = slice(lt * LANES, (lt + 1) * LANES)
            ext_ref[lt, 0:HALO] = zeros
            ext_ref[lt, HALO:2 * HALO] = jnp.where(has_prev, up_ref[bi, :, sl], 0.0)
            ext_ref[lt, 2 * HALO:2 * HALO + TILE] = u_ref[bi, :, sl]
            ext_ref[lt, 2 * HALO + TILE:3 * HALO + TILE] = jnp.where(has_next, un_ref[bi, :, sl], 0.0)
            ext_ref[lt, 3 * HALO + TILE:4 * HALO + TILE] = zeros

    p = lax.broadcasted_iota(jnp.int32, (n_sub * HALO, LRU_GROUP), 0)
    valid_head = LRU_PITCH * (p % n_sub) + p // n_sub >= HALO
    valid_tail = LRU_PITCH * (p % n_sub) + p // n_sub + (LRU_PITCH - HALO) < HALO + TILE

    for cg in range(n_lane_tiles // per_group):
        lo, hi = cg * LRU_GROUP, (cg + 1) * LRU_GROUP
        if reverse:
            uc = ucm_ref[bi, 0, :, lo:hi]
        else:
            for hh in range(per_group):
                lt = cg * per_group + hh
                sl = slice(lt * LANES, (lt + 1) * LANES)
                taps = [cw_ref[j:j + 1, sl] for j in range(CONV_W)]
                bias = cb_ref[:, sl]
                for r in range(LRU_PITCH):
                    acc = None
                    for j in range(CONV_W):
                        x = ext_ref[lt, pl.ds(HALO + r + j - CONV_PAD_LEFT, n_sub, stride=LRU_PITCH), :]
                        acc = x * taps[j] if acc is None else acc + x * taps[j]
                    uc_ref[r * n_sub:(r + 1) * n_sub, hh * LANES:(hh + 1) * LANES] = acc + bias
            uc = uc_ref[...]
            ucm_ref[bi, 0, :, lo:hi] = uc
        pre = jnp.dot(uc.astype(BF16), wg_ref[cg], preferred_element_type=F32)
        tr = jnp.tanh(pre[:, :LRU_GROUP] + 0.5 * bg_ref[0:1, lo:hi])
        ti = jnp.tanh(pre[:, LRU_GROUP:] + 0.5 * bg_ref[1:2, lo:hi])
        half_c = (-0.5 * LRU_C) * _softplus(-lam_ref[:, lo:hi])
        log_a = half_c + half_c * tr
        a = jnp.exp(log_a)
        th = jnp.tanh(log_a)
        zz = (-0.5 * th) / (1.0 - th)
        root = zz * lax.rsqrt(jnp.maximum(zz, F32_MIN_NORMAL))
        bb = (root * uc) * (1.0 + ti)
        head, tail = n_sub * HALO, n_sub * (LRU_PITCH - HALO)
        a_ref[0:head] = jnp.where(valid_head, a[0:head], 1.0)
        b_ref[0:head] = jnp.where(valid_head, bb[0:head], 0.0)
        a_ref[head:tail] = a[head:tail]
        b_ref[head:tail] = bb[head:tail]
        a_ref[tail:] = jnp.where(valid_tail, a[tail:], 1.0)
        b_ref[tail:] = jnp.where(valid_tail, bb[tail:], 0.0)

        for hh in range(per_group):
            lt = cg * per_group + hh
            sl = slice(lt * LANES, (lt + 1) * LANES)
            cols = slice(hh * LANES, (hh + 1) * LANES)
            order = range(LRU_PITCH - 1, -1, -1) if reverse else range(LRU_PITCH)
            h_run = a_run = None
            for r in order:
                rows = slice(r * n_sub, (r + 1) * n_sub)
                a_r = a_ref[rows, cols]
                b_r = b_ref[rows, cols]
                h_run = b_r if h_run is None else a_r * h_run + b_r
                a_run = a_r if a_run is None else a_run * a_r
                b_ref[rows, cols] = h_run
                a_ref[rows, cols] = a_run
            state = carry_ref[bi, :, sl]
            carry_in = [None] * n_sub
            for s in (range(n_sub - 1, -1, -1) if reverse else range(n_sub)):
                carry_in[s] = state
                state = a_run[s:s + 1] * state + h_run[s:s + 1]
            carry_ref[bi, :, sl] = state
            cin = jnp.concatenate(carry_in, axis=0)
            for r in range(LRU_PITCH):
                rows = slice(r * n_sub, (r + 1) * n_sub)
                h_ref[lt, pl.ds(r, n_sub, stride=LRU_PITCH), :] = b_ref[rows, cols] + a_ref[rows, cols] * cin

    for lt in range(n_lane_tiles):
        sl = slice(lt * LANES, (lt + 1) * LANES)
        h = h_ref[lt, HALO:HALO + TILE, :]
        if reverse:
            g = gb_ref[bi, :, sl]
            o_ref[bi, :, sl] = ((hf_ref[bi, :, sl] + h) * (g * _sigmoid(g))).astype(BF16)
        else:
            o_ref[bi, :, sl] = h


def _lru(rest, u_col, gb_col, prm, l, dr, fwd_out, n_ctx_tiles, reverse):
    b, t, _ = rest.shape
    d = prm["conv_w"].shape[2]
    nt = t // TILE
    per_tile = TILE // HALO
    n_halo_blocks = t // HALO
    tile = functools.partial(_lru_tile_index, n_ctx_tiles=n_ctx_tiles, n_tiles=nt, reverse=reverse)
    ucol = u_col // d
    nb = math.gcd(b, LRU_BATCH)

    def tile_spec(col):
        return pl.BlockSpec((nb, TILE, d), lambda i, j: (i, tile(j), col))

    def ucm_spec():
        return pl.BlockSpec((nb, 1, EXT_ROWS, d), lambda i, j: (i, tile(j), 0, 0))

    gate_specs = [_layer_spec(prm["wg"], l, dr), _layer_spec(prm["bg"], l, dr), _layer_spec(prm["lam"], l, dr)]
    gate_args = [prm["wg"], prm["bg"], prm["lam"]]
    scan_scratch = [
        pltpu.VMEM((EXT_ROWS, LRU_GROUP), F32),
        pltpu.VMEM((EXT_ROWS, LRU_GROUP), F32),
        pltpu.VMEM((d // LANES, EXT_ROWS, LANES), F32),
    ]
    carry = pltpu.VMEM((nb, 1, d), F32)
    if reverse:
        hf, ucm = fwd_out
        in_specs = [ucm_spec()] + gate_specs + [tile_spec(0), tile_spec(gb_col // d)]
        args = [ucm] + gate_args + [hf, rest]
        out_specs = tile_spec(0)
        out_shape = jax.ShapeDtypeStruct((b, t, d), BF16)
        scratch = [carry] + scan_scratch
    else:
        in_specs = [
            tile_spec(ucol),
            pl.BlockSpec((nb, HALO, d), lambda i, j: (i, jnp.maximum(tile(j) * per_tile - 1, 0), ucol)),
            pl.BlockSpec((nb, HALO, d),
                         lambda i, j: (i, jnp.minimum((tile(j) + 1) * per_tile, n_halo_blocks - 1), ucol)),
            _layer_spec(prm["conv_w"], l),
            _layer_spec(prm["conv_b"], l),
        ] + gate_specs
        args = [rest, rest, rest, prm["conv_w"], prm["conv_b"]] + gate_args
        out_specs = [tile_spec(0), ucm_spec()]
        out_shape = [jax.ShapeDtypeStruct((b, t, d), F32), jax.ShapeDtypeStruct((b, nt, EXT_ROWS, d), F32)]
        scratch = [
            carry,
            pltpu.VMEM((d // LANES, EXT_ROWS + 2 * HALO, LANES), F32),
            pltpu.VMEM((EXT_ROWS, LRU_GROUP), F32),
        ] + scan_scratch
    kern = functools.partial(_lru_kernel, n_ctx_tiles=n_ctx_tiles, n_tiles=nt, reverse=reverse)
    return pl.pallas_call(
        kern,
        grid=(b // nb, nt),
        in_specs=in_specs,
        out_specs=out_specs,
        out_shape=out_shape,
        scratch_shapes=scratch,
        compiler_params=_params(("parallel", "arbitrary")),
        name="lru_rev" if reverse else "lru_fwd",
    )(*args)


def _merge_kernel(*refs, n_ctx_tiles, skip):
    if skip:
        oa_ref, ob_ref, gm_ref, xc_ref, xl_ref, mod_ref, wa_ref, wb_ref, wo_ref, ol_ref = refs
        oc_ref = None
    else:
        oa_ref, ob_ref, gm_ref, xc_ref, xl_ref, mod_ref, wa_ref, wb_ref, wo_ref, oc_ref, ol_ref = refs
    d = xl_ref.shape[2]
    is_ctx = pl.program_id(1) + skip < n_ctx_tiles

    def one(bi, carry):
        ya = jnp.dot(oa_ref[bi], wa_ref[...], preferred_element_type=F32)
        yb = jnp.dot(ob_ref[bi], wb_ref[...], preferred_element_type=F32)
        mix = _sigmoid(gm_ref[bi, :, 0:d]) * ya + _sigmoid(gm_ref[bi, :, d:2 * d]) * yb
        out = mod_ref[bi, 0, 2:3, :] * jnp.dot(mix.astype(BF16), wo_ref[...], preferred_element_type=F32)
        if oc_ref is None:
            ol_ref[bi] = xl_ref[bi] + out
        else:
            ol_ref[bi] = jnp.where(is_ctx, xc_ref[bi], xl_ref[bi]) + out

            @pl.when(is_ctx)
            def _():
                oc_ref[bi] = ol_ref[bi]
        return carry

    lax.fori_loop(0, xl_ref.shape[0], one, 0)


def _merge(oa, ob, rest, xc, xl, prm, l, latent_only):
    b, n_ctx, d = xc.shape
    n_ctx_tiles = n_ctx // TILE
    nt = n_ctx_tiles + xl.shape[1] // TILE
    skip = n_ctx_tiles if latent_only else 0
    nb = math.gcd(b, MERGE_BATCH)
    ctx_spec, lat_spec = _stream_specs(nb, d, n_ctx_tiles, skip)
    out_specs = _stream_specs(nb, d, n_ctx_tiles, skip)
    out_shape = [jax.ShapeDtypeStruct(xc.shape, F32), jax.ShapeDtypeStruct(xl.shape, F32)]
    if latent_only:
        out_specs, out_shape = out_specs[1:], out_shape[1:]
    outs = pl.pallas_call(
        functools.partial(_merge_kernel, n_ctx_tiles=n_ctx_tiles, skip=skip),
        grid=(b // nb, nt - skip),
        in_specs=[
            pl.BlockSpec((nb, TILE, d), lambda i, j: (i, j + skip, 0)),
            pl.BlockSpec((nb, TILE, d), lambda i, j: (i, j + skip, 0)),
            pl.BlockSpec((nb, TILE, 2 * d), lambda i, j: (i, j + skip, 0)),
            ctx_spec,
            lat_spec,
            pl.BlockSpec((None, nb, 1, 3, d),
                         lambda i, j: (l, i, jnp.where(j + skip < n_ctx_tiles, 0, 1), 0, 0)),
            _layer_spec(prm["wa"], l),
            _layer_spec(prm["wb"], l),
            _layer_spec(prm["wo"], l),
        ],
        out_specs=out_specs,
        out_shape=out_shape,
        compiler_params=_params(("parallel", "arbitrary")),
        name="merge",
    )(oa, ob, rest, xc, xl, prm["modsel"], prm["wa"], prm["wb"], prm["wo"])
    return (None, outs[0]) if latent_only else tuple(outs)


def _rope_tables(n, n_ctx):
    quarter = HEAD_DIM // 4
    freqs = ROPE_THETA ** (-jnp.arange(quarter, dtype=F32) / quarter)
    pos = jnp.arange(n)
    ang_r = (pos // GRID_W).astype(F32)[None, :] * freqs[:, None]
    ang_c = (pos % GRID_W).astype(F32)[None, :] * freqs[:, None]
    cos = jnp.concatenate([jnp.cos(ang_r)] * 2 + [jnp.cos(ang_c)] * 2, axis=0)
    sgn = jnp.concatenate([-jnp.sin(ang_r), jnp.sin(ang_r), -jnp.sin(ang_c), jnp.sin(ang_c)], axis=0)
    cos = jnp.concatenate([jnp.ones((HEAD_DIM, n_ctx), F32), cos], axis=1)
    sgn = jnp.concatenate([jnp.zeros((HEAD_DIM, n_ctx), F32), sgn], axis=1)
    return cos, sgn


def _gate_weights(w):
    lead = w.shape[:-4]
    blk = w.shape[-1]
    per = LRU_GROUP // blk
    n_groups = LRU_BLOCKS // per
    wg = w.reshape(lead + (2, n_groups, per, blk, blk))
    eye = jnp.eye(per, dtype=w.dtype)
    dense = jnp.einsum('...gnpij,pq->...gnpiqj', wg, eye).reshape(lead + (2, n_groups, per * blk, per * blk))
    return (0.5 * jnp.concatenate([dense[..., 0, :, :, :], dense[..., 1, :, :, :]], axis=-1)).astype(BF16)


def kernel(x, c, ctx, c_ctx, norm_g, w_mod, b_mod, w_in, q_norm_g, k_norm_g, conv_w, conv_b,
           lru_gate_w, lru_gate_b, lru_lambda, w_a_out, w_b_out, w_out):
    bsz, n, d = x.shape
    n_ctx = ctx.shape[1]
    depth = w_in.shape[0]
    assert n % TILE == 0 and n_ctx % TILE == 0 and n % GRID_W == 0
    d_attn = N_Q_HEADS * HEAD_DIM
    d_kv = N_KV_HEADS * HEAD_DIM
    d_lru = conv_w.shape[-1]
    assert d_lru == d and d_attn == d and w_in.shape[2] == 2 * d_attn + 2 * d_kv + 2 * d_lru + 2 * d
    n_ctx_tiles = n_ctx // TILE

    rows = -(-(bsz + 1) // 8) * 8
    cond = jnp.zeros((rows, d), F32).at[:bsz].set(c).at[bsz].set(c_ctx)
    mod = _modulation(cond, w_mod, b_mod)
    cos_t, sgn_t = _rope_tables(n, n_ctx)

    o_q = 0
    o_ga = d_attn + 2 * d_kv
    o_u = o_ga + d_attn
    o_gb = o_u + d_lru
    o_gm = o_gb + d_lru
    ga_col, u_col, gb_col = 2 * d, 2 * d + d_attn, 2 * d + d_attn + d_lru

    m3 = mod.reshape(depth, rows, 3, d)
    modsel = jnp.stack([jnp.broadcast_to(m3[:, bsz:bsz + 1], (depth, bsz, 3, d)), m3[:, :bsz]],
                       axis=2)
    wt = jnp.swapaxes(w_in[:, :, o_q:o_ga], 1, 2).astype(BF16)
    wn = jnp.concatenate([w_in[:, :, o_gm:], w_in[:, :, o_ga:o_u], w_in[:, :, o_u:o_gb],
                          w_in[:, :, o_gb:o_gm]], axis=2).astype(BF16)
    def swap(g):
        return g.reshape(depth, 2, 2, HEAD_DIM // 4)[:, :, ::-1].reshape(depth, HEAD_DIM)

    q_scale = HEAD_DIM ** -0.5 * LOG2_E
    rope = jnp.stack([(q_scale * q_norm_g)[:, :, None] * cos_t, (q_scale * swap(q_norm_g))[:, :, None] * sgn_t,
                      k_norm_g[:, :, None] * cos_t, swap(k_norm_g)[:, :, None] * sgn_t],
                     axis=1)
    prm = dict(
        ng=norm_g[:, None, :], wt=wt, wn=wn, rope=rope,
        conv_w=conv_w, conv_b=conv_b[:, None, :], wg=_gate_weights(lru_gate_w), bg=lru_gate_b,
        lam=lru_lambda[:, :, None, :], wa=w_a_out.astype(BF16), wb=w_b_out.astype(BF16),
        wo=w_out.astype(BF16), modsel=modsel)

    xc, xl = ctx, x
    for l in range(depth):
        qt, k4, vt, rest = _in_proj(xc, xl, prm, l)
        oa = _attention(qt, k4, vt, rest, ga_col, n_ctx_tiles)
        hf = None
        for dr, rev in enumerate((False, True)):
            hf = _lru(rest, u_col, gb_col, prm, l, dr, hf, n_ctx_tiles, rev)
        xc, xl = _merge(oa, hf, rest, xc, xl, prm, l, latent_only=(l == depth - 1))
    return xl
```

```python
import functools
import math

import jax
import jax.numpy as jnp
from jax import lax
from jax.experimental import pallas as pl
from jax.experimental.pallas import tpu as pltpu

GRID_W = 64
HEAD_DIM = 64
N_Q_HEADS = 16
N_KV_HEADS = 4
Q_PER_KV = N_Q_HEADS // N_KV_HEADS
LRU_BLOCKS = 16
CONV_W = 4
CONV_PAD_LEFT = 2
LRU_C = 8.0
ROPE_THETA = 10000.0
EPS = 1e-6

TILE = 256
HALO = 8
LANES = 128
LRU_GROUP = 256
EXT_ROWS = TILE + 2 * HALO
LRU_PITCH = EXT_ROWS // 8
assert LRU_PITCH * 8 == EXT_ROWS and LRU_PITCH % 4 != 0
IN_PROJ_BATCH = 2
MERGE_BATCH = 2
LRU_BATCH = 4
SUM_ROWS = 16
LOG2_E = 1.4426950408889634
F32_MIN_NORMAL = 1.1754943508222875e-38
VMEM_LIMIT = 56 * 1024 * 1024

F32 = jnp.float32
BF16 = jnp.bfloat16


def _params(sem, vmem=VMEM_LIMIT):
    return pltpu.CompilerParams(dimension_semantics=sem, vmem_limit_bytes=vmem)


def _layer_spec(arr, *lead):
    tail = arr.shape[len(lead):]
    index = tuple(lead) + (0,) * len(tail)
    return pl.BlockSpec((None,) * len(lead) + tuple(tail), lambda *_: index, pipeline_mode=pl.Buffered(1))


def _mod_kernel(c_ref, w_ref, b_ref, o_ref):
    s = jax.nn.silu(c_ref[...]).astype(BF16)
    o_ref[0] = jnp.dot(s, w_ref[0].astype(BF16), preferred_element_type=F32) + b_ref[0]


def _modulation(cond, w_mod, b_mod):
    depth, d, d3 = w_mod.shape
    rows = cond.shape[0]
    nchunk = d3 // d
    return pl.pallas_call(
        _mod_kernel,
        grid=(depth, nchunk),
        in_specs=[
            pl.BlockSpec((rows, d), lambda l, j: (0, 0)),
            pl.BlockSpec((1, d, d), lambda l, j: (l, 0, j)),
            pl.BlockSpec((1, 1, d), lambda l, j: (l, 0, j)),
        ],
        out_specs=pl.BlockSpec((1, rows, d), lambda l, j: (l, 0, j)),
        out_shape=jax.ShapeDtypeStruct((depth, rows, d3), F32),
        compiler_params=_params(("arbitrary", "arbitrary")),
        name="modulation",
    )(cond, w_mod, b_mod.reshape(depth, 1, d3))


def _head_norm_rope(z, tab_a, tab_b):
    h, hd, tm = z.shape
    ms = jnp.mean(z * z, axis=1, keepdims=True)
    y = z * lax.rsqrt(ms + EPS)
    quarter = hd // 4
    y4 = y.reshape(h * 2, 2, quarter, tm)
    swapped = jnp.concatenate([y4[:, 1:2], y4[:, 0:1]], axis=1).reshape(h, hd, tm)
    return y * tab_a + swapped * tab_b


def _in_proj_kernel(xc_ref, xl_ref, mod_ref, ng_ref, wt_ref, wn_ref, tab_ref,
                    q_ref, k_ref, v_ref, rest_ref, *, n_ctx_tiles):
    d_attn = N_Q_HEADS * HEAD_DIM
    d_kv = N_KV_HEADS * HEAD_DIM
    width = Q_PER_KV * HEAD_DIM
    tm = xl_ref.shape[1]
    n_rest = wn_ref.shape[1]
    is_ctx = pl.program_id(1) < n_ctx_tiles

    def one(bi, carry):
        x = jnp.where(is_ctx, xc_ref[bi], xl_ref[bi])
        ms = jnp.mean(x * x, axis=-1, keepdims=True)
        y = x * lax.rsqrt(ms + EPS) * ng_ref[...]
        shift = mod_ref[bi, 0, 0:1, :]
        scale = mod_ref[bi, 0, 1:2, :]
        h = (y * (1.0 + scale) + shift).astype(BF16)

        pt = lax.dot_general(wt_ref[...], h, (((1,), (1,)), ((), ())), preferred_element_type=F32)
        q = _head_norm_rope(pt[0:d_attn].reshape(N_Q_HEADS, HEAD_DIM, tm), tab_ref[0], tab_ref[1])
        qs = q.reshape(d_attn, tm).astype(BF16)
        for j in range(N_KV_HEADS):
            q_ref[bi, j, 0] = qs[j * width:(j + 1) * width]
        k = _head_norm_rope(pt[d_attn:d_attn + d_kv].reshape(N_KV_HEADS, HEAD_DIM, tm), tab_ref[2], tab_ref[3])
        kn = k.reshape(d_kv, tm).T
        for j in range(N_KV_HEADS):
            k_ref[bi, j] = kn[:, j * HEAD_DIM:(j + 1) * HEAD_DIM].astype(BF16)
            v_ref[bi, j, 0] = pt[d_attn + d_kv + j * HEAD_DIM:d_attn + d_kv + (j + 1) * HEAD_DIM].astype(BF16)

        step = 1024
        for c in range(0, n_rest, step):
            rest_ref[bi, :, c:c + step] = jnp.dot(h, wn_ref[:, c:c + step], preferred_element_type=F32)
        return carry

    for bi in range(xl_ref.shape[0]):
        one(bi, 0)


def _stream_specs(nb, d, n_ctx_tiles, skip=0):
    return [
        pl.BlockSpec((nb, TILE, d), lambda i, j: (i, jnp.minimum(j + skip, n_ctx_tiles - 1), 0)),
        pl.BlockSpec((nb, TILE, d), lambda i, j: (i, jnp.maximum(j + skip - n_ctx_tiles, 0), 0)),
    ]


def _in_proj(xc, xl, prm, l):
    b, n_ctx, d = xc.shape
    n_ctx_tiles = n_ctx // TILE
    t = n_ctx + xl.shape[1]
    nt = t // TILE
    d_attn = N_Q_HEADS * HEAD_DIM
    n_rest = prm["wn"].shape[2]
    nb = math.gcd(b, IN_PROJ_BATCH)
    return pl.pallas_call(
        functools.partial(_in_proj_kernel, n_ctx_tiles=n_ctx_tiles),
        grid=(b // nb, nt),
        in_specs=_stream_specs(nb, d, n_ctx_tiles) + [
            pl.BlockSpec((None, nb, 1, 3, d), lambda i, j: (l, i, jnp.where(j < n_ctx_tiles, 0, 1), 0, 0)),
            _layer_spec(prm["ng"], l),
            _layer_spec(prm["wt"], l),
            _layer_spec(prm["wn"], l),
            pl.BlockSpec((None, 4, HEAD_DIM, TILE), lambda i, j: (l, 0, 0, j)),
        ],
        out_specs=[
            pl.BlockSpec((nb, N_KV_HEADS, 1, d_attn // N_KV_HEADS, TILE), lambda i, j: (i, 0, j, 0, 0)),
            pl.BlockSpec((nb, N_KV_HEADS, TILE, HEAD_DIM), lambda i, j: (i, 0, j, 0)),
            pl.BlockSpec((nb, N_KV_HEADS, 1, HEAD_DIM, TILE), lambda i, j: (i, 0, j, 0, 0)),
            pl.BlockSpec((nb, TILE, n_rest), lambda i, j: (i, j, 0)),
        ],
        out_shape=[
            jax.ShapeDtypeStruct((b, N_KV_HEADS, nt, d_attn // N_KV_HEADS, TILE), BF16),
            jax.ShapeDtypeStruct((b, N_KV_HEADS, t, HEAD_DIM), BF16),
            jax.ShapeDtypeStruct((b, N_KV_HEADS, nt, HEAD_DIM, TILE), BF16),
            jax.ShapeDtypeStruct((b, t, n_rest), F32),
        ],
        compiler_params=_params(("parallel", "parallel")),
        name="in_proj",
    )(xc, xl, prm["modsel"], prm["ng"], prm["wt"], prm["wn"], prm["rope"])


def _attn_kernel(q_ref, k_ref, v_ref, ga_ref, o_ref, s_ref, mx_ref, m_ref, acc_ref,
                 *, n_ctx_tiles, n_tiles):
    ones_rows = jnp.ones((SUM_ROWS, TILE), BF16)

    def scores(qt, c, slot):
        kc = k_ref[0, 0, c * TILE:(c + 1) * TILE, :]
        for g in range(Q_PER_KV):
            qg = q_ref[0, 0, qt, g * HEAD_DIM:(g + 1) * HEAD_DIM, :]
            s = jnp.dot(kc, qg, preferred_element_type=F32)
            s_ref[slot, g] = s
            mx_ref[slot, g] = jnp.max(s, axis=0, keepdims=True)

    def update(c, slot, acc_set, first):
        vc = jnp.concatenate([v_ref[0, 0, c], ones_rows], axis=0)
        for g in range(Q_PER_KV):
            if first:
                m_new = mx_ref[slot, g]
                p = jnp.exp2(s_ref[slot, g] - m_new)
                acc_ref[acc_set, g] = jnp.dot(vc, p.astype(BF16), preferred_element_type=F32)
            else:
                m_prev = m_ref[acc_set, g]
                m_new = jnp.maximum(m_prev, mx_ref[slot, g])
                alpha = jnp.exp2(m_prev - m_new)
                p = jnp.exp2(s_ref[slot, g] - m_new)
                acc_ref[acc_set, g] = (alpha * acc_ref[acc_set, g]
                                       + jnp.dot(vc, p.astype(BF16), preferred_element_type=F32))
            m_ref[acc_set, g] = m_new

    def finish(qt, acc_set):
        o = jnp.concatenate([acc_ref[acc_set, g, 0:HEAD_DIM] / acc_ref[acc_set, g, HEAD_DIM:HEAD_DIM + 1]
                             for g in range(Q_PER_KV)], axis=0)
        if isinstance(qt, int):
            rows = slice(qt * TILE, (qt + 1) * TILE)
        else:
            rows = pl.ds(pl.multiple_of(qt * TILE, TILE), TILE)
        ga = ga_ref[0, rows, :]
        o_ref[0, rows, :] = (o.T * (ga * _sigmoid(ga))).astype(BF16)

    def tile(qt, parity, chunks, next_qt, next_chunk):
        for idx, c in enumerate(chunks):
            slot = (parity + idx) % 2
            if idx + 1 < len(chunks):
                scores(qt, chunks[idx + 1], 1 - slot)
            else:
                scores(next_qt, next_chunk, 1 - slot)
            update(c, slot, parity, idx == 0)
        finish(qt, parity)

    ctx_chunks = list(range(n_ctx_tiles))
    all_chunks = list(range(n_tiles))
    parity = 0
    scores(0, 0, parity)
    for qt in range(n_ctx_tiles):
        tile(qt, parity, ctx_chunks, qt + 1, 0)
        parity = (parity + len(ctx_chunks)) % 2
    n_latent = n_tiles - n_ctx_tiles
    pairs = n_latent // 2
    odd = len(all_chunks) % 2
    first_parity = parity

    def pair(i, carry):
        qa = n_ctx_tiles + 2 * i
        tile(qa, first_parity, all_chunks, qa + 1, 0)
        tile(qa + 1, (first_parity + odd) % 2, all_chunks, jnp.minimum(qa + 2, n_tiles - 1), 0)
        return carry

    lax.fori_loop(0, pairs, pair, 0)


def _attention(q5, k4, vt, rest, ga_col, n_ctx_tiles):
    b, _, nt, width, _ = q5.shape
    t = nt * TILE
    d_attn = N_KV_HEADS * width
    assert (nt - n_ctx_tiles) % 2 == 0
    kern = functools.partial(_attn_kernel, n_ctx_tiles=n_ctx_tiles, n_tiles=nt)
    return pl.pallas_call(
        kern,
        grid=(b, N_KV_HEADS),
        in_specs=[
            pl.BlockSpec((1, 1, nt, width, TILE), lambda i, j: (i, j, 0, 0, 0)),
            pl.BlockSpec((1, 1, t, HEAD_DIM), lambda i, j: (i, j, 0, 0)),
            pl.BlockSpec((1, 1, nt, HEAD_DIM, TILE), lambda i, j: (i, j, 0, 0, 0)),
            pl.BlockSpec((1, t, width), lambda i, j: (i, 0, ga_col // width + j)),
        ],
        out_specs=pl.BlockSpec((1, t, width), lambda i, j: (i, 0, j)),
        out_shape=jax.ShapeDtypeStruct((b, t, d_attn), BF16),
        scratch_shapes=[
            pltpu.VMEM((2, Q_PER_KV, TILE, TILE), F32),
            pltpu.VMEM((2, Q_PER_KV, 1, TILE), F32),
            pltpu.VMEM((2, Q_PER_KV, 1, TILE), F32),
            pltpu.VMEM((2, Q_PER_KV, HEAD_DIM + SUM_ROWS, TILE), F32),
        ],
        compiler_params=_params(("parallel", "parallel")),
        name="attention",
    )(q5, k4, vt, rest)


def _softplus(x):
    return jnp.maximum(x, 0.0) + jnp.log1p(jnp.exp(-jnp.abs(x)))


def _sigmoid(x):
    return 0.5 + 0.5 * jnp.tanh(0.5 * x)


def _lru_tile_index(i, n_ctx_tiles, n_tiles, reverse):
    if not reverse:
        return i
    return jnp.where(i < n_ctx_tiles, n_ctx_tiles - 1 - i, n_tiles - 1 - (i - n_ctx_tiles))


def _lru_kernel(*refs, n_ctx_tiles, n_tiles, reverse):
    carry_ref = refs[7 if reverse else 10]

    @pl.when(pl.program_id(1) == 0)
    def _():
        carry_ref[...] = jnp.zeros(carry_ref.shape, F32)

    def one(bi, carry):
        _lru_tile(bi, refs, n_ctx_tiles, n_tiles, reverse)
        return carry

    lax.fori_loop(0, refs[0].shape[0], one, 0)


def _lru_tile(bi, refs, n_ctx_tiles, n_tiles, reverse):
    if reverse:
        (ucm_ref, wg_ref, bg_ref, lam_ref, hf_ref, gb_ref,
         o_ref, carry_ref, a_ref, b_ref, h_ref) = refs
        d = ucm_ref.shape[3]
    else:
        (u_ref, up_ref, un_ref, cw_ref, cb_ref, wg_ref, bg_ref, lam_ref,
         o_ref, ucm_ref, carry_ref, ext_ref, uc_ref, a_ref, b_ref, h_ref) = refs
        d = u_ref.shape[2]
    tb = _lru_tile_index(pl.program_id(1), n_ctx_tiles, n_tiles, reverse)
    n_lane_tiles = d // LANES
    per_group = LRU_GROUP // LANES
    n_sub = EXT_ROWS // LRU_PITCH
    if not reverse:
        has_prev = jnp.logical_and(tb != 0, tb != n_ctx_tiles)
        has_next = jnp.logical_and(tb != n_ctx_tiles - 1, tb != n_tiles - 1)
        zeros = jnp.zeros((HALO, LANES), F32)
        for lt in range(n_lane_tiles):
            sl = slice(lt * LANES, (lt + 1) * LANES)
            ext_ref[lt, 0:HALO] = zeros
            ext_ref[lt, HALO:2 * HALO] = jnp.where(has_prev, up_ref[bi, :, sl], 0.0)
            ext_ref[lt, 2 * HALO:2 * HALO + TILE] = u_ref[bi, :, sl]
            ext_ref[lt, 2 * HALO + TILE:3 * HALO + TILE] = jnp.where(has_next, un_ref[bi, :, sl], 0.0)
            ext_ref[lt, 3 * HALO + TILE:4 * HALO + TILE] = zeros

    p = lax.broadcasted_iota(jnp.int32, (n_sub * HALO, LRU_GROUP), 0)
    valid_head = LRU_PITCH * (p % n_sub) + p // n_sub >= HALO
    valid_tail = LRU_PITCH * (p % n_sub) + p // n_sub + (LRU_PITCH - HALO) < HALO + TILE

    for cg in range(n_lane_tiles // per_group):
        lo, hi = cg * LRU_GROUP, (cg + 1) * LRU_GROUP
        if reverse:
            uc = ucm_ref[bi, 0, :, lo:hi]
        else:
            for hh in range(per_group):
                lt = cg * per_group + hh
                sl = slice(lt * LANES, (lt + 1) * LANES)
                taps = [cw_ref[j:j + 1, sl] for j in range(CONV_W)]
                bias = cb_ref[:, sl]
                for r in range(LRU_PITCH):
                    acc = None
                    for j in range(CONV_W):
                        x = ext_ref[lt, pl.ds(HALO + r + j - CONV_PAD_LEFT, n_sub, stride=LRU_PITCH), :]
                        acc = x * taps[j] if acc is None else acc + x * taps[j]
                    uc_ref[r * n_sub:(r + 1) * n_sub, hh * LANES:(hh + 1) * LANES] = acc + bias
            uc = uc_ref[...]
            ucm_ref[bi, 0, :, lo:hi] = uc
        pre = jnp.dot(uc.astype(BF16), wg_ref[cg], preferred_element_type=F32)
        tr = jnp.tanh(pre[:, :LRU_GROUP] + 0.5 * bg_ref[0:1, lo:hi])
        ti = jnp.tanh(pre[:, LRU_GROUP:] + 0.5 * bg_ref[1:2, lo:hi])
        half_c = (-0.5 * LRU_C) * _softplus(-lam_ref[:, lo:hi])
        log_a = half_c + half_c * tr
        a = jnp.exp(log_a)
        th = jnp.tanh(log_a)
        zz = (-0.5 * th) / (1.0 - th)
        root = zz * lax.rsqrt(jnp.maximum(zz, F32_MIN_NORMAL))
        bb = (root * uc) * (1.0 + ti)
        head, tail = n_sub * HALO, n_sub * (LRU_PITCH - HALO)
        a_ref[0:head] = jnp.where(valid_head, a[0:head], 1.0)
        b_ref[0:head] = jnp.where(valid_head, bb[0:head], 0.0)
        a_ref[head:tail] = a[head:tail]
        b_ref[head:tail] = bb[head:tail]
        a_ref[tail:] = jnp.where(valid_tail, a[tail:], 1.0)
        b_ref[tail:] = jnp.where(valid_tail, bb[tail:], 0.0)

        for hh in range(per_group):
            lt = cg * per_group + hh
            sl = slice(lt * LANES, (lt + 1) * LANES)
            cols = slice(hh * LANES, (hh + 1) * LANES)
            order = range(LRU_PITCH - 1, -1, -1) if reverse else range(LRU_PITCH)
            h_run = a_run = None
            for r in order:
                rows = slice(r * n_sub, (r + 1) * n_sub)
                a_r = a_ref[rows, cols]
                b_r = b_ref[rows, cols]
                h_run = b_r if h_run is None else a_r * h_run + b_r
                a_run = a_r if a_run is None else a_run * a_r
                b_ref[rows, cols] = h_run
                a_ref[rows, cols] = a_run
            state = carry_ref[bi, :, sl]
            carry_in = [None] * n_sub
            for s in (range(n_sub - 1, -1, -1) if reverse else range(n_sub)):
                carry_in[s] = state
                state = a_run[s:s + 1] * state + h_run[s:s + 1]
            carry_ref[bi, :, sl] = state
            cin = jnp.concatenate(carry_in, axis=0)
            for r in range(LRU_PITCH):
                rows = slice(r * n_sub, (r + 1) * n_sub)
                h_ref[lt, pl.ds(r, n_sub, stride=LRU_PITCH), :] = b_ref[rows, cols] + a_ref[rows, cols] * cin

    for lt in range(n_lane_tiles):
        sl = slice(lt * LANES, (lt + 1) * LANES)
        h = h_ref[lt, HALO:HALO + TILE, :]
        if reverse:
            g = gb_ref[bi, :, sl]
            o_ref[bi, :, sl] = ((hf_ref[bi, :, sl] + h) * (g * _sigmoid(g))).astype(BF16)
        else:
            o_ref[bi, :, sl] = h


def _lru(rest, u_col, gb_col, prm, l, dr, fwd_out, n_ctx_tiles, reverse):
    b, t, _ = rest.shape
    d = prm["conv_w"].shape[2]
    nt = t // TILE
    per_tile = TILE // HALO
    n_halo_blocks = t // HALO
    tile = functools.partial(_lru_tile_index, n_ctx_tiles=n_ctx_tiles, n_tiles=nt, reverse=reverse)
    ucol = u_col // d
    nb = math.gcd(b, LRU_BATCH)

    def tile_spec(col):
        return pl.BlockSpec((nb, TILE, d), lambda i, j: (i, tile(j), col))

    def ucm_spec():
        return pl.BlockSpec((nb, 1, EXT_ROWS, d), lambda i, j: (i, tile(j), 0, 0))

    gate_specs = [_layer_spec(prm["wg"], l, dr), _layer_spec(prm["bg"], l, dr), _layer_spec(prm["lam"], l, dr)]
    gate_args = [prm["wg"], prm["bg"], prm["lam"]]
    scan_scratch = [
        pltpu.VMEM((EXT_ROWS, LRU_GROUP), F32),
        pltpu.VMEM((EXT_ROWS, LRU_GROUP), F32),
        pltpu.VMEM((d // LANES, EXT_ROWS, LANES), F32),
    ]
    carry = pltpu.VMEM((nb, 1, d), F32)
    if reverse:
        hf, ucm = fwd_out
        in_specs = [ucm_spec()] + gate_specs + [tile_spec(0), tile_spec(gb_col // d)]
        args = [ucm] + gate_args + [hf, rest]
        out_specs = tile_spec(0)
        out_shape = jax.ShapeDtypeStruct((b, t, d), BF16)
        scratch = [carry] + scan_scratch
    else:
        in_specs = [
            tile_spec(ucol),
            pl.BlockSpec((nb, HALO, d), lambda i, j: (i, jnp.maximum(tile(j) * per_tile - 1, 0), ucol)),
            pl.BlockSpec((nb, HALO, d),
                         lambda i, j: (i, jnp.minimum((tile(j) + 1) * per_tile, n_halo_blocks - 1), ucol)),
            _layer_spec(prm["conv_w"], l),
            _layer_spec(prm["conv_b"], l),
        ] + gate_specs
        args = [rest, rest, rest, prm["conv_w"], prm["conv_b"]] + gate_args
        out_specs = [tile_spec(0), ucm_spec()]
        out_shape = [jax.ShapeDtypeStruct((b, t, d), F32), jax.ShapeDtypeStruct((b, nt, EXT_ROWS, d), F32)]
        scratch = [
            carry,
            pltpu.VMEM((d // LANES, EXT_ROWS + 2 * HALO, LANES), F32),
            pltpu.VMEM((EXT_ROWS, LRU_GROUP), F32),
        ] + scan_scratch
    kern = functools.partial(_lru_kernel, n_ctx_tiles=n_ctx_tiles, n_tiles=nt, reverse=reverse)
    return pl.pallas_call(
        kern,
        grid=(b // nb, nt),
        in_specs=in_specs,
        out_specs=out_specs,
        out_shape=out_shape,
        scratch_shapes=scratch,
        compiler_params=_params(("parallel", "arbitrary")),
        name="lru_rev" if reverse else "lru_fwd",
    )(*args)


def _merge_kernel(*refs, n_ctx_tiles, skip):
    if skip:
        oa_ref, ob_ref, gm_ref, xc_ref, xl_ref, mod_ref, wa_ref, wb_ref, wo_ref, ol_ref = refs
        oc_ref = None
    else:
        oa_ref, ob_ref, gm_ref, xc_ref, xl_ref, mod_ref, wa_ref, wb_ref, wo_ref, oc_ref, ol_ref = refs
    d = xl_ref.shape[2]
    is_ctx = pl.program_id(1) + skip < n_ctx_tiles

    def one(bi, carry):
        ya = jnp.dot(oa_ref[bi], wa_ref[...], preferred_element_type=F32)
        yb = jnp.dot(ob_ref[bi], wb_ref[...], preferred_element_type=F32)
        mix = _sigmoid(gm_ref[bi, :, 0:d]) * ya + _sigmoid(gm_ref[bi, :, d:2 * d]) * yb
        out = mod_ref[bi, 0, 2:3, :] * jnp.dot(mix.astype(BF16), wo_ref[...], preferred_element_type=F32)
        if oc_ref is None:
            ol_ref[bi] = xl_ref[bi] + out
        else:
            ol_ref[bi] = jnp.where(is_ctx, xc_ref[bi], xl_ref[bi]) + out

            @pl.when(is_ctx)
            def _():
                oc_ref[bi] = ol_ref[bi]
        return carry

    lax.fori_loop(0, xl_ref.shape[0], one, 0)


def _merge(oa, ob, rest, xc, xl, prm, l, latent_only):
    b, n_ctx, d = xc.shape
    n_ctx_tiles = n_ctx // TILE
    nt = n_ctx_tiles + xl.shape[1] // TILE
    skip = n_ctx_tiles if latent_only else 0
    nb = math.gcd(b, MERGE_BATCH)
    ctx_spec, lat_spec = _stream_specs(nb, d, n_ctx_tiles, skip)
    out_specs = _stream_specs(nb, d, n_ctx_tiles, skip)
    out_shape = [jax.ShapeDtypeStruct(xc.shape, F32), jax.ShapeDtypeStruct(xl.shape, F32)]
    if latent_only:
        out_specs, out_shape = out_specs[1:], out_shape[1:]
    outs = pl.pallas_call(
        functools.partial(_merge_kernel, n_ctx_tiles=n_ctx_tiles, skip=skip),
        grid=(b // nb, nt - skip),
        in_specs=[
            pl.BlockSpec((nb, TILE, d), lambda i, j: (i, j + skip, 0)),
            pl.BlockSpec((nb, TILE, d), lambda i, j: (i, j + skip, 0)),
            pl.BlockSpec((nb, TILE, 2 * d), lambda i, j: (i, j + skip, 0)),
            ctx_spec,
            lat_spec,
            pl.BlockSpec((None, nb, 1, 3, d),
                         lambda i, j: (l, i, jnp.where(j + skip < n_ctx_tiles, 0, 1), 0, 0)),
            _layer_spec(prm["wa"], l),
            _layer_spec(prm["wb"], l),
            _layer_spec(prm["wo"], l),
        ],
        out_specs=out_specs,
        out_shape=out_shape,
        compiler_params=_params(("parallel", "arbitrary")),
        name="merge",
    )(oa, ob, rest, xc, xl, prm["modsel"], prm["wa"], prm["wb"], prm["wo"])
    return (None, outs[0]) if latent_only else tuple(outs)


def _rope_tables(n, n_ctx):
    quarter = HEAD_DIM // 4
    freqs = ROPE_THETA ** (-jnp.arange(quarter, dtype=F32) / quarter)
    pos = jnp.arange(n)
    ang_r = (pos // GRID_W).astype(F32)[None, :] * freqs[:, None]
    ang_c = (pos % GRID_W).astype(F32)[None, :] * freqs[:, None]
    cos = jnp.concatenate([jnp.cos(ang_r)] * 2 + [jnp.cos(ang_c)] * 2, axis=0)
    sgn = jnp.concatenate([-jnp.sin(ang_r), jnp.sin(ang_r), -jnp.sin(ang_c), jnp.sin(ang_c)], axis=0)
    cos = jnp.concatenate([jnp.ones((HEAD_DIM, n_ctx), F32), cos], axis=1)
    sgn = jnp.concatenate([jnp.zeros((HEAD_DIM, n_ctx), F32), sgn], axis=1)
    return cos, sgn


def _gate_weights(w):
    lead = w.shape[:-4]
    blk = w.shape[-1]
    per = LRU_GROUP // blk
    n_groups = LRU_BLOCKS // per
    wg = w.reshape(lead + (2, n_groups, per, blk, blk))
    eye = jnp.eye(per, dtype=w.dtype)
    dense = jnp.einsum('...gnpij,pq->...gnpiqj', wg, eye).reshape(lead + (2, n_groups, per * blk, per * blk))
    return (0.5 * jnp.concatenate([dense[..., 0, :, :, :], dense[..., 1, :, :, :]], axis=-1)).astype(BF16)


def kernel(x, c, ctx, c_ctx, norm_g, w_mod, b_mod, w_in, q_norm_g, k_norm_g, conv_w, conv_b,
           lru_gate_w, lru_gate_b, lru_lambda, w_a_out, w_b_out, w_out):
    bsz, n, d = x.shape
    n_ctx = ctx.shape[1]
    depth = w_in.shape[0]
    assert n % TILE == 0 and n_ctx % TILE == 0 and n % GRID_W == 0
    d_attn = N_Q_HEADS * HEAD_DIM
    d_kv = N_KV_HEADS * HEAD_DIM
    d_lru = conv_w.shape[-1]
    assert d_lru == d and d_attn == d and w_in.shape[2] == 2 * d_attn + 2 * d_kv + 2 * d_lru + 2 * d
    n_ctx_tiles = n_ctx // TILE

    rows = -(-(bsz + 1) // 8) * 8
    cond = jnp.zeros((rows, d), F32).at[:bsz].set(c).at[bsz].set(c_ctx)
    mod = _modulation(cond, w_mod, b_mod)
    cos_t, sgn_t = _rope_tables(n, n_ctx)

    o_q = 0
    o_ga = d_attn + 2 * d_kv
    o_u = o_ga + d_attn
    o_gb = o_u + d_lru
    o_gm = o_gb + d_lru
    ga_col, u_col, gb_col = 2 * d, 2 * d + d_attn, 2 * d + d_attn + d_lru

    m3 = mod.reshape(depth, rows, 3, d)
    modsel = jnp.stack([jnp.broadcast_to(m3[:, bsz:bsz + 1], (depth, bsz, 3, d)), m3[:, :bsz]],
                       axis=2)
    w_bf = w_in.astype(BF16)
    wt = jnp.swapaxes(w_bf[:, :, o_q:o_ga], 1, 2)
    wn = jnp.concatenate([w_bf[:, :, o_gm:], w_bf[:, :, o_ga:o_u], w_bf[:, :, o_u:o_gb],
                          w_bf[:, :, o_gb:o_gm]], axis=2)
    def swap(g):
        return g.reshape(depth, 2, 2, HEAD_DIM // 4)[:, :, ::-1].reshape(depth, HEAD_DIM)

    q_scale = HEAD_DIM ** -0.5 * LOG2_E
    rope = jnp.stack([(q_scale * q_norm_g)[:, :, None] * cos_t, (q_scale * swap(q_norm_g))[:, :, None] * sgn_t,
                      k_norm_g[:, :, None] * cos_t, swap(k_norm_g)[:, :, None] * sgn_t],
                     axis=1)
    prm = dict(
        ng=norm_g[:, None, :], wt=wt, wn=wn, rope=rope,
        conv_w=conv_w, conv_b=conv_b[:, None, :], wg=_gate_weights(lru_gate_w), bg=lru_gate_b,
        lam=lru_lambda[:, :, None, :], wa=w_a_out.astype(BF16), wb=w_b_out.astype(BF16),
        wo=w_out.astype(BF16), modsel=modsel)

    xc, xl = ctx, x
    for l in range(depth):
        qt, k4, vt, rest = _in_proj(xc, xl, prm, l)
        oa = _attention(qt, k4, vt, rest, ga_col, n_ctx_tiles)
        hf = None
        for dr, rev in enumerate((False, True)):
            hf = _lru(rest, u_col, gb_col, prm, l, dr, hf, n_ctx_tiles, rev)
        xc, xl = _merge(oa, hf, rest, xc, xl, prm, l, latent_only=(l == depth - 1))
    return xl
```
